```python
import math
import jax, jax.numpy as jnp
from jax import lax
import numpy as np

D_MODEL = 2048
BATCH = 1
SEQ = 8192
DEPTH = 4

CHUNK = 64
A_HEADS = 8
A_KDIM = 128
A_VDIM = 128
A_KWIDTH = A_HEADS * A_KDIM
A_VWIDTH = A_HEADS * A_VDIM
B_GROUPS = 64
B_GROUP_CH = 16
B_STATE = 64
B_WIDTH = B_GROUPS * B_GROUP_CH
S5_DT_MIN = 1e-3
S5_DT_MAX = 1e-1
S5_EIG_CLIP = -1e-4
IN_SIZES = (A_KWIDTH, A_KWIDTH, A_VWIDTH, A_VWIDTH, B_WIDTH, D_MODEL, D_MODEL)
IN_COLS = sum(IN_SIZES)
IN_SPLITS = tuple(int(v) for v in np.cumsum(IN_SIZES)[:-1])
P_HEADS = 8
P_QDIM = 256
P_HALF = P_QDIM // 2
P_NKEYS = 128
P_NEXP = P_NKEYS * P_NKEYS
P_TOPK = 16
P_BLOCK = 128
ALPHA = (2.0 * DEPTH) ** 0.25
BETA = (8.0 * DEPTH) ** -0.25
LN_EPS = 1e-5
RMS_EPS = 1e-6

kernel_name = 'hybrid_hgrn2_s5_peer_deepnorm'


def layer_norm(x, g, b):
    xf = x.astype(jnp.float32)
    mu = jnp.mean(xf, axis=-1, keepdims=True)
    var = jnp.mean(jnp.square(xf - mu), axis=-1, keepdims=True)
    y = (xf - mu) * lax.rsqrt(var + LN_EPS) * g.astype(jnp.float32) + b.astype(jnp.float32)
    return y.astype(x.dtype)


def hgrn2_lower_bounds(lb_logits):
    p = jax.nn.softmax(lb_logits.astype(jnp.float32), axis=0)
    c = jnp.cumsum(p, axis=0)
    return c - c[0:1]


def hgrn2_mixer(q, fz, i, g, lb, norm_g):
    bsz, L, _ = q.shape
    nc = L // CHUNK
    f32 = jnp.float32
    q = q.astype(f32); fz = fz.astype(f32); i = i.astype(f32)
    logf = jnp.logaddexp(jnp.log(lb), jnp.log1p(-lb) + jax.nn.log_sigmoid(fz))
    k = -jnp.expm1(logf)

    def to_chunks(t, d):
        return jnp.moveaxis(t.reshape(bsz, nc, CHUNK, A_HEADS, d), 1, 0)

    qc = to_chunks(q, A_KDIM)
    kc = to_chunks(k, A_KDIM)
    vc = to_chunks(i, A_VDIM)
    bc = jnp.cumsum(to_chunks(logf, A_KDIM), axis=2)
    mask = jnp.tril(jnp.ones((CHUNK, CHUNK), dtype=bool))[None, :, :, None, None]

    def step(S, inp):
        qq, kk, vv, bb = inp
        b_last = bb[:, -1]
        o_inter = jnp.einsum('bthd,bhdv->bthv', qq * jnp.exp(bb), S)
        diff = bb[:, :, None] - bb[:, None, :]
        decay = jnp.exp(jnp.where(mask, diff, -jnp.inf))
        att = jnp.sum(qq[:, :, None] * kk[:, None, :] * decay, axis=-1)
        o_intra = jnp.einsum('btsh,bshv->bthv', att, vv)
        k_tail = kk * jnp.exp(b_last[:, None] - bb)
        S = jnp.exp(b_last)[..., None] * S + jnp.einsum('bshd,bshv->bhdv', k_tail, vv)
        return S, o_inter + o_intra

    S0 = jnp.zeros((bsz, A_HEADS, A_KDIM, A_VDIM), f32)
    _, o = lax.scan(step, S0, (qc, kc, vc, bc))
    o = jnp.moveaxis(o, 0, 1).reshape(bsz, L, A_HEADS, A_VDIM)
    o = o * lax.rsqrt(jnp.mean(jnp.square(o), axis=-1, keepdims=True) + RMS_EPS)
    o = o * norm_g.astype(f32).reshape(A_HEADS, A_VDIM)
    o = o.reshape(bsz, L, A_VWIDTH) * jax.nn.sigmoid(g.astype(f32))
    return o


def _complex_affine_combine(e1, e2):
    a1r, a1i, b1r, b1i = e1
    a2r, a2i, b2r, b2i = e2
    ar = a2r * a1r - a2i * a1i
    ai = a2r * a1i + a2i * a1r
    br = a2r * b1r - a2i * b1i + b2r
    bi = a2r * b1i + a2i * b1r + b2i
    return ar, ai, br, bi


def s5_mixer(u, lam_re, lam_im, log_step, b_re, b_im, c_re, c_im, d, w_glu):
    bsz, L, _ = u.shape
    f32 = jnp.float32
    uf = u.astype(f32).reshape(bsz, L, B_GROUPS, B_GROUP_CH)
    lr = jnp.minimum(lam_re.astype(f32), S5_EIG_CLIP)
    li = lam_im.astype(f32)
    dt = jnp.exp(log_step.astype(f32))[:, None]
    mag = jnp.exp(lr * dt)
    ar = mag * jnp.cos(li * dt)
    ai = mag * jnp.sin(li * dt)
    den = lr * lr + li * li
    nr = ar - 1.0
    zr = (nr * lr + ai * li) / den
    zi = (ai * lr - nr * li) / den
    br_, bi_ = b_re.astype(f32), b_im.astype(f32)
    bbr = zr[..., None] * br_ - zi[..., None] * bi_
    bbi = zr[..., None] * bi_ + zi[..., None] * br_
    bur = jnp.einsum('blgn,gpn->blgp', uf, bbr)
    bui = jnp.einsum('blgn,gpn->blgp', uf, bbi)
    a_r = jnp.broadcast_to(ar, bur.shape)
    a_i = jnp.broadcast_to(ai, bui.shape)
    _, _, xr, xi = lax.associative_scan(_complex_affine_combine, (a_r, a_i, bur, bui), axis=1)
    y = (jnp.einsum('blgp,gnp->blgn', xr, c_re.astype(f32))
         - jnp.einsum('blgp,gnp->blgn', xi, c_im.astype(f32))
         + d.astype(f32) * uf)
    y = jax.nn.gelu(y).reshape(bsz, L, B_WIDTH)
    h = y @ w_glu.astype(f32)
    return h[..., :B_WIDTH] * jax.nn.sigmoid(h[..., B_WIDTH:])


def peer_ffn(x, w_q, keys, u_tab, v_tab):
    bsz, L, _ = x.shape
    f32 = jnp.float32
    q = (x @ w_q).astype(f32).reshape(bsz, L, P_HEADS, P_QDIM)
    kf = keys.astype(f32)
    s1 = jnp.einsum('blhd,hnd->blhn', q[..., :P_HALF], kf[:, 0])
    s2 = jnp.einsum('blhd,hnd->blhn', q[..., P_HALF:], kf[:, 1])
    v1, i1 = lax.top_k(s1, P_TOPK)
    v2, i2 = lax.top_k(s2, P_TOPK)
    cand = (v1[..., :, None] + v2[..., None, :]).reshape(bsz, L, P_HEADS, P_TOPK * P_TOPK)
    cid = (i1[..., :, None] * P_NKEYS + i2[..., None, :]).reshape(bsz, L, P_HEADS, P_TOPK * P_TOPK)
    vals, pos = lax.top_k(cand, P_TOPK)
    eid = jnp.take_along_axis(cid, pos, axis=-1)
    gate = jax.nn.softmax(vals, axis=-1).astype(x.dtype)
    nb = L // P_BLOCK

    def blocks(t):
        return jnp.moveaxis(t.reshape((bsz, nb, P_BLOCK) + t.shape[2:]), 1, 0)

    def block_fn(args):
        xb, eb, gb = args
        ue = jnp.take(u_tab, eb, axis=0)
        ve = jnp.take(v_tab, eb, axis=0)
        act = jax.nn.gelu(jnp.einsum('btd,bthkd->bthk', xb, ue)) * gb
        return jnp.einsum('bthk,bthkd->btd', act, ve).astype(x.dtype)

    y = lax.map(block_fn, (blocks(x), blocks(eid), blocks(gate)))
    return jnp.moveaxis(y, 0, 1).reshape(bsz, L, D_MODEL)


def setup_inputs(seed: int = 0) -> dict:
    key = jax.random.key(seed)
    ks = jax.random.split(key, 24)
    f32 = jnp.float32
    nrm = lambda k, s, sc: jax.random.normal(k, s, f32) * sc
    n_idx = jnp.arange(B_STATE, dtype=f32) * math.pi
    return {
        'x': nrm(ks[0], (BATCH, SEQ, D_MODEL), 1.0),
        'w_in': nrm(ks[1], (DEPTH, D_MODEL, IN_COLS), D_MODEL ** -0.5),
        'hgrn_lb_logits': nrm(ks[2], (DEPTH, A_KWIDTH), 0.1),
        'hgrn_norm_g': 1.0 + nrm(ks[3], (DEPTH, A_VWIDTH), 0.02),
        's5_lambda_re': -0.5 + nrm(ks[4], (DEPTH, B_GROUPS, B_STATE), 0.01),
        's5_lambda_im': n_idx + nrm(ks[5], (DEPTH, B_GROUPS, B_STATE), 0.01),
        's5_log_step': jax.random.uniform(ks[6], (DEPTH, B_GROUPS), f32, math.log(S5_DT_MIN), math.log(S5_DT_MAX)),
        's5_b_re': nrm(ks[7], (DEPTH, B_GROUPS, B_STATE, B_GROUP_CH), (2.0 * B_GROUP_CH) ** -0.5),
        's5_b_im': nrm(ks[8], (DEPTH, B_GROUPS, B_STATE, B_GROUP_CH), (2.0 * B_GROUP_CH) ** -0.5),
        's5_c_re': nrm(ks[9], (DEPTH, B_GROUPS, B_GROUP_CH, B_STATE), (2.0 * B_STATE) ** -0.5),
        's5_c_im': nrm(ks[10], (DEPTH, B_GROUPS, B_GROUP_CH, B_STATE), (2.0 * B_STATE) ** -0.5),
        's5_d': nrm(ks[11], (DEPTH, B_GROUPS, B_GROUP_CH), 1.0),
        's5_w_glu': nrm(ks[12], (DEPTH, B_WIDTH, 2 * B_WIDTH), B_WIDTH ** -0.5),
        'w_up_a': nrm(ks[13], (DEPTH, A_VWIDTH, D_MODEL), BETA * A_VWIDTH ** -0.5),
        'w_up_b': nrm(ks[14], (DEPTH, B_WIDTH, D_MODEL), BETA * B_WIDTH ** -0.5),
        'w_o': nrm(ks[15], (DEPTH, D_MODEL, D_MODEL), BETA * D_MODEL ** -0.5),
        'ln1_g': 1.0 + nrm(ks[16], (DEPTH, D_MODEL), 0.02),
        'ln1_b': nrm(ks[17], (DEPTH, D_MODEL), 0.02),
        'peer_w_q': nrm(ks[18], (DEPTH, D_MODEL, P_HEADS * P_QDIM), D_MODEL ** -0.5),
        'peer_keys': nrm(ks[19], (DEPTH, P_HEADS, 2, P_NKEYS, P_HALF), P_HALF ** -0.5),
        'peer_u': nrm(ks[20], (DEPTH, P_NEXP, D_MODEL), D_MODEL ** -0.5),
        'peer_v': nrm(ks[21], (DEPTH, P_NEXP, D_MODEL), BETA * D_MODEL ** -0.5),
        'ln2_g': 1.0 + nrm(ks[22], (DEPTH, D_MODEL), 0.02),
        'ln2_b': nrm(ks[23], (DEPTH, D_MODEL), 0.02),
    }


def reference(x, w_in, hgrn_lb_logits, hgrn_norm_g, s5_lambda_re, s5_lambda_im, s5_log_step,
              s5_b_re, s5_b_im, s5_c_re, s5_c_im, s5_d, s5_w_glu, w_up_a, w_up_b, w_o,
              ln1_g, ln1_b, peer_w_q, peer_keys, peer_u, peer_v, ln2_g, ln2_b):
    lbs = hgrn2_lower_bounds(hgrn_lb_logits)
    for l in range(DEPTH):
        proj = x @ w_in[l]
        qa, fa, ia, ga, ub, gate_a, gate_b = jnp.split(proj, IN_SPLITS, axis=-1)
        oa = hgrn2_mixer(qa, fa, ia, ga, lbs[l], hgrn_norm_g[l]).astype(x.dtype)
        ob = s5_mixer(ub, s5_lambda_re[l], s5_lambda_im[l], s5_log_step[l], s5_b_re[l], s5_b_im[l],
                      s5_c_re[l], s5_c_im[l], s5_d[l], s5_w_glu[l]).astype(x.dtype)
        merged = (jax.nn.sigmoid(gate_a) * (oa @ w_up_a[l])
                  + jax.nn.sigmoid(gate_b) * (ob @ w_up_b[l]))
        x = layer_norm(ALPHA * x + merged @ w_o[l], ln1_g[l], ln1_b[l])
        y = peer_ffn(x, peer_w_q[l], peer_keys[l], peer_u[l], peer_v[l])
        x = layer_norm(ALPHA * x + y, ln2_g[l], ln2_b[l])
    return x
```

```python
import functools
import math

import jax
import jax.numpy as jnp
import numpy as np
from jax import lax
from jax.experimental import pallas as pl
from jax.experimental.pallas import tpu as pltpu

D_MODEL = 2048
DEPTH = 4
A_HEADS = 8
A_DIM = 128
A_WIDTH = A_HEADS * A_DIM
HG_CHUNK = 16
B_GROUPS = 64
B_GROUP_CH = 16
B_STATE = 64
B_WIDTH = B_GROUPS * B_GROUP_CH
S5_CHUNK = 16
S5_ROW = S5_CHUNK * B_GROUP_CH
S5_EIG_CLIP = -1e-4
IN_COLS = 4 * A_WIDTH + B_WIDTH + 2 * D_MODEL
P_HEADS = 8
P_QDIM = 256
P_HALF = 128
P_NKEYS = 128
P_NEXP = P_NKEYS * P_NKEYS
P_TOPK = 16
ALPHA = (2.0 * DEPTH) ** 0.25
LN_EPS = 1e-5
RMS_EPS = 1e-6

LANES = 128
VMEM_LIMIT = 52 * 1024 * 1024

_NT = (((1,), (1,)), ((), ()))
_TN = (((0,), (0,)), ((), ()))


def _cparams(*sem):
    return pltpu.CompilerParams(dimension_semantics=sem, vmem_limit_bytes=VMEM_LIMIT)


def _bf(x):
    return x.astype(jnp.bfloat16)


def _dot(a, b):
    return jnp.dot(a, b, preferred_element_type=jnp.float32)


def _sigmoid(x):
    return 1.0 / (1.0 + jnp.exp(-x))


def _gelu(x):
    c = math.sqrt(2.0 / math.pi)
    return 0.5 * x * (1.0 + jnp.tanh(c * (x + 0.044715 * (x * x * x))))


def _layer_norm(z, g, b):
    mu = jnp.mean(z, axis=-1, keepdims=True)
    zc = z - mu
    var = jnp.mean(zc * zc, axis=-1, keepdims=True)
    return zc * lax.rsqrt(var + LN_EPS) * g + b


def _matmul_kernel(a_ref, w_ref, o_ref):
    o_ref[...] = _dot(a_ref[...], w_ref[...]).astype(o_ref.dtype)


def _matmul(a, w, out_dtype, tm, tn):
    m, k = a.shape
    n = w.shape[1]
    tm = min(tm, m)
    return pl.pallas_call(
        _matmul_kernel,
        grid=(n // tn, m // tm),
        in_specs=[pl.BlockSpec((tm, k), lambda j, i: (i, 0)),
                  pl.BlockSpec((k, tn), lambda j, i: (0, j))],
        out_specs=pl.BlockSpec((tm, tn), lambda j, i: (i, j)),
        out_shape=jax.ShapeDtypeStruct((m, n), out_dtype),
        compiler_params=_cparams("parallel", "parallel"),
        name="matmul",
    )(a, w)


def _hgrn2_kernel(q_ref, f_ref, i_ref, g_ref, loglb_ref, log1mlb_ref, ng_ref, tri_ref,
                  o_ref, st_ref, b_ref, k_ref):
    tb = q_ref.shape[0]

    @pl.when(pl.program_id(1) == 0)
    def _():
        st_ref[...] = jnp.zeros_like(st_ref)

    fz = f_ref[...]
    log_sig = jnp.minimum(fz, 0.0) - jnp.log1p(jnp.exp(-jnp.abs(fz)))
    z = log1mlb_ref[...] + log_sig
    a = loglb_ref[...]
    logf = jnp.maximum(a, z) + jnp.log1p(jnp.exp(-jnp.abs(a - z)))
    k_ref[...] = 1.0 - jnp.exp(logf)
    tri = tri_ref[...]
    for r in range(tb // LANES):
        rows = slice(r * LANES, (r + 1) * LANES)
        lf = logf[rows]
        hi = _bf(lf)
        r1 = lf - hi.astype(jnp.float32)
        mid = _bf(r1)
        lo = _bf(r1 - mid.astype(jnp.float32))
        b_ref[rows, :] = _dot(tri, hi) + _dot(tri, mid) + _dot(tri, lo)

    srow = lax.broadcasted_iota(jnp.int32, (HG_CHUNK, 1), 0)
    ng = ng_ref[...]

    def chunk(j, carry):
        rows = pl.ds(pl.multiple_of(j * HG_CHUNK, HG_CHUNK), HG_CHUNK)
        b = b_ref[rows, :]
        q = q_ref[rows, :]
        k = k_ref[rows, :]
        v = i_ref[rows, :]
        b_last = b[HG_CHUNK - 1:HG_CHUNK, :]
        st = st_ref[...]
        o = lax.dot_general(_bf(q * jnp.exp(b)), _bf(st), _NT,
                            preferred_element_type=jnp.float32)
        intra = []
        for t in range(HG_CHUNK):
            diff = jnp.where(srow <= t, b[t:t + 1, :] - b, -jnp.inf)
            p = (q[t:t + 1, :] * k) * jnp.exp(diff)
            att = jnp.sum(p, axis=-1, keepdims=True)
            intra.append(jnp.sum(att * v, axis=0, keepdims=True))
        o = o + jnp.concatenate(intra, axis=0)
        k_tail = k * jnp.exp(b_last - b)
        st_ref[...] = st * jnp.exp(b_last) + lax.dot_general(
            _bf(v), _bf(k_tail), _TN, preferred_element_type=jnp.float32)
        o = o * lax.rsqrt(jnp.mean(o * o, axis=-1, keepdims=True) + RMS_EPS)
        o = o * ng * _sigmoid(g_ref[rows, :])
        o_ref[rows, :] = o.astype(o_ref.dtype)
        return carry

    lax.fori_loop(0, tb // HG_CHUNK, chunk, 0)


def _hgrn2(proj, loglb, log1mlb, norm_g, tri, tb):
    n_tok = proj.shape[0]
    tb = min(tb, n_tok)

    def col(off):
        return pl.BlockSpec((tb, A_DIM), lambda h, i: (i, off + h))

    def par():
        return pl.BlockSpec((1, A_DIM), lambda h, i: (0, h))

    return pl.pallas_call(
        _hgrn2_kernel,
        grid=(A_HEADS, n_tok // tb),
        in_specs=[col(0), col(A_HEADS), col(2 * A_HEADS), col(3 * A_HEADS),
                  par(), par(), par(),
                  pl.BlockSpec((LANES, LANES), lambda h, i: (0, 0))],
        out_specs=pl.BlockSpec((tb, A_DIM), lambda h, i: (i, h)),
        out_shape=jax.ShapeDtypeStruct((n_tok, A_WIDTH), jnp.bfloat16),
        scratch_shapes=[pltpu.VMEM((A_DIM, A_DIM), jnp.float32),
                        pltpu.VMEM((tb, A_DIM), jnp.float32),
                        pltpu.VMEM((tb, A_DIM), jnp.float32)],
        compiler_params=_cparams("parallel", "arbitrary"),
        name="hgrn2",
    )(proj, proj, proj, proj, loglb, log1mlb, norm_g, tri)


def _s5_tables(lam_re, lam_im, log_step, b_re, b_im, c_re, c_im, d):
    f32 = jnp.float32
    hp = lax.Precision.HIGHEST
    t = S5_CHUNK
    lr = jnp.minimum(lam_re.astype(f32), S5_EIG_CLIP)
    li = lam_im.astype(f32)
    dt = jnp.exp(log_step.astype(f32))[:, None]
    mag = jnp.exp(lr * dt)
    ar = mag * jnp.cos(li * dt)
    ai = mag * jnp.sin(li * dt)
    den = lr * lr + li * li
    nr = ar - 1.0
    zr = (nr * lr + ai * li) / den
    zi = (ai * lr - nr * li) / den
    br_, bi_ = b_re.astype(f32), b_im.astype(f32)
    bbr = zr[..., None] * br_ - zi[..., None] * bi_
    bbi = zr[..., None] * bi_ + zi[..., None] * br_
    pr = [jnp.ones_like(ar)]
    pi = [jnp.zeros_like(ai)]
    for _ in range(t):
        r0, i0 = pr[-1], pi[-1]
        pr.append(r0 * ar - i0 * ai)
        pi.append(r0 * ai + i0 * ar)
    pwr = jnp.stack(pr, axis=1)
    pwi = jnp.stack(pi, axis=1)
    cr, ci = c_re.astype(f32), c_im.astype(f32)
    car = cr[:, None] * pwr[:, :t, None, :] - ci[:, None] * pwi[:, :t, None, :]
    cai = cr[:, None] * pwi[:, :t, None, :] + ci[:, None] * pwr[:, :t, None, :]
    kk = (jnp.einsum('gtnp,gpm->gtnm', car, bbr, precision=hp)
          - jnp.einsum('gtnp,gpm->gtnm', cai, bbi, precision=hp))
    ti = np.arange(t)
    lag = ti[None, :] - ti[:, None]
    toep = kk[:, np.clip(lag, 0, t - 1)]
    toep = jnp.where((lag >= 0)[None, :, :, None, None], toep, 0.0)
    toep = jnp.transpose(toep, (0, 1, 4, 2, 3)).reshape(B_GROUPS, S5_ROW, S5_ROW)
    er = pwr[:, t - 1 - ti][:, :, None, :]
    ei = pwi[:, t - 1 - ti][:, :, None, :]
    bbr_t = jnp.transpose(bbr, (0, 2, 1))[:, None]
    bbi_t = jnp.transpose(bbi, (0, 2, 1))[:, None]
    w1 = jnp.concatenate([er * bbr_t - ei * bbi_t, er * bbi_t + ei * bbr_t], axis=-1)
    w1 = w1.reshape(B_GROUPS, S5_ROW, 2 * B_STATE)
    zr1 = pwr[:, 1:, None, :]
    zi1 = pwi[:, 1:, None, :]
    w2r = cr[:, None] * zr1 - ci[:, None] * zi1
    w2i = -(cr[:, None] * zi1 + ci[:, None] * zr1)
    w2 = jnp.concatenate([w2r, w2i], axis=-1)
    w2 = jnp.transpose(w2, (0, 3, 1, 2)).reshape(B_GROUPS, 2 * B_STATE, S5_ROW)
    a1 = jnp.concatenate([pwr[:, t], pwr[:, t]], axis=-1)
    a2 = jnp.concatenate([-pwi[:, t], pwi[:, t]], axis=-1)
    dd = jnp.tile(d.astype(f32), (1, t))[:, None, :]
    return _bf(toep), _bf(w1), _bf(w2), a1, a2, dd


def _s5_local_kernel(u_ref, w1_ref, e_ref):
    e_ref[...] = _dot(_bf(u_ref[0]), w1_ref[0])


def _s5_scan_kernel(e_ref, a1_ref, a2_ref, xp_ref, st_ref):
    @pl.when(pl.program_id(0) == 0)
    def _():
        st_ref[...] = jnp.zeros_like(st_ref)

    a1 = a1_ref[...]
    a2 = a2_ref[...]

    def step(c, carry):
        x = st_ref[...]
        xp_ref[c] = x
        st_ref[...] = a1 * x + a2 * pltpu.roll(x, B_STATE, 1) + e_ref[c]
        return carry

    lax.fori_loop(0, e_ref.shape[0], step, 0)


def _s5_out_kernel(u_ref, xp_ref, toep_ref, w2_ref, d_ref, y_ref):
    u = u_ref[0]
    y = (_dot(_bf(u), toep_ref[0]) + _dot(_bf(xp_ref[...]), w2_ref[0]) + d_ref[0] * u)
    y_ref[0] = _gelu(y).astype(y_ref.dtype)


def _s5(ut, tables, cb):
    toep, w1, w2, a1, a2, dd = tables
    g, nc, _ = ut.shape
    ns = 2 * B_STATE
    cb = min(cb, nc)
    e = pl.pallas_call(
        _s5_local_kernel,
        grid=(g,),
        in_specs=[pl.BlockSpec((1, nc, S5_ROW), lambda i: (i, 0, 0)),
                  pl.BlockSpec((1, S5_ROW, ns), lambda i: (i, 0, 0))],
        out_specs=pl.BlockSpec((nc, ns), lambda i: (0, i)),
        out_shape=jax.ShapeDtypeStruct((nc, g * ns), jnp.float32),
        compiler_params=_cparams("parallel"),
        name="s5_local",
    )(ut, w1)
    xp = pl.pallas_call(
        _s5_scan_kernel,
        grid=(nc // cb,),
        in_specs=[pl.BlockSpec((cb, g, ns), lambda i: (i, 0, 0)),
                  pl.BlockSpec((g, ns), lambda i: (0, 0)),
                  pl.BlockSpec((g, ns), lambda i: (0, 0))],
        out_specs=pl.BlockSpec((cb, g, ns), lambda i: (i, 0, 0)),
        out_shape=jax.ShapeDtypeStruct((nc, g, ns), jnp.float32),
        scratch_shapes=[pltpu.VMEM((g, ns), jnp.float32)],
        compiler_params=_cparams("arbitrary"),
        name="s5_scan",
    )(e.reshape(nc, g, ns), a1, a2)
    return pl.pallas_call(
        _s5_out_kernel,
        grid=(g,),
        in_specs=[pl.BlockSpec((1, nc, S5_ROW), lambda i: (i, 0, 0)),
                  pl.BlockSpec((nc, ns), lambda i: (0, i)),
                  pl.BlockSpec((1, S5_ROW, S5_ROW), lambda i: (i, 0, 0)),
                  pl.BlockSpec((1, ns, S5_ROW), lambda i: (i, 0, 0)),
                  pl.BlockSpec((1, 1, S5_ROW), lambda i: (i, 0, 0))],
        out_specs=pl.BlockSpec((1, nc, S5_ROW), lambda i: (i, 0, 0)),
        out_shape=jax.ShapeDtypeStruct((g, nc, S5_ROW), jnp.bfloat16),
        compiler_params=_cparams("parallel"),
        name="s5_out",
    )(ut, xp.reshape(nc, g * ns), toep, w2, dd)


def _glu_kernel(y_ref, wa_ref, wb_ref, o_ref):
    y = y_ref[...]
    o_ref[...] = (_dot(y, wa_ref[...]) * _sigmoid(_dot(y, wb_ref[...]))).astype(o_ref.dtype)


def _glu(y, w, tm, tn):
    m, k = y.shape
    n = w.shape[1] // 2
    tm = min(tm, m)
    nb = n // tn
    return pl.pallas_call(
        _glu_kernel,
        grid=(nb, m // tm),
        in_specs=[pl.BlockSpec((tm, k), lambda j, i: (i, 0)),
                  pl.BlockSpec((k, tn), lambda j, i: (0, j)),
                  pl.BlockSpec((k, tn), lambda j, i: (0, nb + j))],
        out_specs=pl.BlockSpec((tm, tn), lambda j, i: (i, j)),
        out_shape=jax.ShapeDtypeStruct((m, n), jnp.bfloat16),
        compiler_params=_cparams("parallel", "parallel"),
        name="glu",
    )(y, w, w)


def _up_merge_kernel(oa_ref, ob_ref, wa_ref, wb_ref, ga_ref, gb_ref, o_ref):
    m = (_sigmoid(ga_ref[...]) * _dot(oa_ref[...], wa_ref[...])
         + _sigmoid(gb_ref[...]) * _dot(ob_ref[...], wb_ref[...]))
    o_ref[...] = m.astype(o_ref.dtype)


def _up_merge(oa, ob, wa, wb, proj, tm, tn):
    m, k = oa.shape
    n = wa.shape[1]
    tm = min(tm, m)
    ga0 = (4 * A_WIDTH + B_WIDTH) // tn
    gb0 = ga0 + D_MODEL // tn
    return pl.pallas_call(
        _up_merge_kernel,
        grid=(n // tn, m // tm),
        in_specs=[pl.BlockSpec((tm, k), lambda j, i: (i, 0)),
                  pl.BlockSpec((tm, k), lambda j, i: (i, 0)),
                  pl.BlockSpec((k, tn), lambda j, i: (0, j)),
                  pl.BlockSpec((k, tn), lambda j, i: (0, j)),
                  pl.BlockSpec((tm, tn), lambda j, i: (i, ga0 + j)),
                  pl.BlockSpec((tm, tn), lambda j, i: (i, gb0 + j))],
        out_specs=pl.BlockSpec((tm, tn), lambda j, i: (i, j)),
        out_shape=jax.ShapeDtypeStruct((m, n), jnp.bfloat16),
        compiler_params=_cparams("parallel", "parallel"),
        name="up_merge",
    )(oa, ob, wa, wb, proj, proj)


def _wo_ln_kernel(m_ref, w_ref, x_ref, g_ref, b_ref, o_ref, ob_ref):
    z = ALPHA * x_ref[...] + _dot(m_ref[...], w_ref[...])
    y = _layer_norm(z, g_ref[...], b_ref[...])
    o_ref[...] = y
    ob_ref[...] = _bf(y)


def _wo_ln(merged, w, x, g, b, tm):
    m, k = merged.shape
    n = w.shape[1]
    tm = min(tm, m)
    row = lambda i: (i, 0)
    fix = lambda i: (0, 0)
    return pl.pallas_call(
        _wo_ln_kernel,
        grid=(m // tm,),
        in_specs=[pl.BlockSpec((tm, k), row), pl.BlockSpec((k, n), fix),
                  pl.BlockSpec((tm, n), row), pl.BlockSpec((1, n), fix),
                  pl.BlockSpec((1, n), fix)],
        out_specs=[pl.BlockSpec((tm, n), row), pl.BlockSpec((tm, n), row)],
        out_shape=[jax.ShapeDtypeStruct((m, n), jnp.float32),
                   jax.ShapeDtypeStruct((m, n), jnp.bfloat16)],
        compiler_params=_cparams("parallel"),
        name="wo_ln",
    )(merged, w, x, g, b)


def _add_ln_kernel(x_ref, y_ref, g_ref, b_ref, o_ref, ob_ref):
    y = _layer_norm(ALPHA * x_ref[...] + y_ref[...], g_ref[...], b_ref[...])
    o_ref[...] = y
    ob_ref[...] = _bf(y)


def _add_ln(x, y, g, b, tm):
    m, n = x.shape
    tm = min(tm, m)
    row = lambda i: (i, 0)
    fix = lambda i: (0, 0)
    return pl.pallas_call(
        _add_ln_kernel,
        grid=(m // tm,),
        in_specs=[pl.BlockSpec((tm, n), row), pl.BlockSpec((tm, n), row),
                  pl.BlockSpec((1, n), fix), pl.BlockSpec((1, n), fix)],
        out_specs=[pl.BlockSpec((tm, n), row), pl.BlockSpec((tm, n), row)],
        out_shape=[jax.ShapeDtypeStruct((m, n), jnp.float32),
                   jax.ShapeDtypeStruct((m, n), jnp.bfloat16)],
        compiler_params=_cparams("parallel"),
        name="add_ln",
    )(x, y, g, b)


def _take_top(s, n_take, on_take):
    rows = s.shape[0]
    rid = lax.broadcasted_iota(jnp.int32, s.shape, 0)
    for k in range(n_take):
        m = jnp.max(s, axis=0, keepdims=True)
        idx = jnp.min(jnp.where(s == m, rid, rows), axis=0, keepdims=True)
        on_take(k, m, idx)
        s = jnp.where(rid == idx, -jnp.inf, s)


def _peer_route_kernel(q_ref, keys_ref, r2_ref, e2_ref, n_ref, c_ref):
    tt = q_ref.shape[0]
    kid = lax.broadcasted_iota(jnp.int32, (P_NKEYS, tt), 0)
    rank_id = lax.broadcasted_iota(jnp.int32, (P_TOPK, tt), 0)
    for h in range(P_HEADS):
        s = []
        for half in range(2):
            qh = q_ref[:, (2 * h + half) * P_HALF:(2 * h + half + 1) * P_HALF]
            s.append(lax.dot_general(_bf(keys_ref[h, half]), _bf(qh), _NT,
                                     preferred_element_type=jnp.float32))
        vals = [[], []]
        idxs = [[], []]
        rank2 = [jnp.full((P_NKEYS, tt), P_TOPK, jnp.int32)]
        for half in range(2):
            def take(k, m, idx, half=half):
                vals[half].append(m)
                idxs[half].append(idx)
                if half == 1:
                    rank2[0] = jnp.where(kid == idx, k, rank2[0])
            _take_top(s[half], P_TOPK, take)
        v1 = jnp.concatenate(vals[0], axis=0)
        v2 = jnp.concatenate(vals[1], axis=0)
        cand = jnp.concatenate([v1[i:i + 1] + v2 for i in range(P_TOPK)], axis=0)
        state = [jnp.zeros((P_TOPK, tt), jnp.int32), jnp.zeros((1, tt), jnp.float32)]
        top = v1[0:1] + v2[0:1]

        def take_c(k, m, pos):
            state[0] = state[0] + (rank_id == (pos >> 4)).astype(jnp.int32)
            state[1] = state[1] + jnp.exp(m - top)
        _take_top(cand, P_TOPK, take_c)
        n_rank, z = state
        n_key = jnp.zeros((P_NKEYS, tt), jnp.int32)
        for i in range(P_TOPK):
            n_key = jnp.where(kid == idxs[0][i], n_rank[i:i + 1], n_key)
        r2_ref[h] = rank2[0].astype(jnp.float32)
        n_ref[h] = n_key.astype(jnp.float32)
        e2_ref[h] = jnp.exp(s[1] - v2[0:1])
        c_ref[h] = jnp.exp(s[0] - v1[0:1]) / z


def _peer_route(q, keys, tt):
    n_tok = q.shape[0]
    tt = min(tt, n_tok)
    out = jax.ShapeDtypeStruct((P_HEADS, P_NKEYS, n_tok), jnp.float32)
    ospec = pl.BlockSpec((P_HEADS, P_NKEYS, tt), lambda i: (0, 0, i))
    return pl.pallas_call(
        _peer_route_kernel,
        grid=(n_tok // tt,),
        in_specs=[pl.BlockSpec((tt, P_HEADS * P_QDIM), lambda i: (i, 0)),
                  pl.BlockSpec((P_HEADS, 2, P_NKEYS, P_HALF), lambda i: (0, 0, 0, 0))],
        out_specs=[ospec] * 4,
        out_shape=[out] * 4,
        compiler_params=_cparams("parallel"),
        name="peer_route",
    )(q, keys)


def _peer_ffn_kernel(x_ref, u_ref, vt_ref, r2_ref, e2_ref, n_ref, c_ref, y_ref, acc_ref, h_ref):
    j = pl.program_id(1)
    eb = u_ref.shape[0]

    @pl.when(j == 0)
    def _():
        acc_ref[...] = jnp.zeros_like(acc_ref)

    act = lax.dot_general(u_ref[...], x_ref[...], _NT, preferred_element_type=jnp.float32)
    for al in range(eb // P_NKEYS):
        a = j * (eb // P_NKEYS) + al
        gate = jnp.zeros((P_NKEYS, act.shape[1]), jnp.float32)
        for h in range(P_HEADS):
            n_row = n_ref[h, pl.ds(a, 1), :]
            c_row = c_ref[h, pl.ds(a, 1), :]
            gate = gate + jnp.where(r2_ref[h] < n_row, e2_ref[h] * c_row, 0.0)
        rows = slice(al * P_NKEYS, (al + 1) * P_NKEYS)
        h_ref[rows, :] = (_gelu(act[rows]) * gate).astype(h_ref.dtype)
    acc_ref[...] += _dot(vt_ref[...], h_ref[...])

    @pl.when(j == pl.num_programs(1) - 1)
    def _():
        y_ref[...] = acc_ref[...].T


def _peer_ffn(xb, u, vt, route, tm, eb):
    n_tok, d = xb.shape
    tm = min(tm, n_tok)
    rspec = pl.BlockSpec((P_HEADS, P_NKEYS, tm), lambda i, j: (0, 0, i))
    return pl.pallas_call(
        _peer_ffn_kernel,
        grid=(n_tok // tm, P_NEXP // eb),
        in_specs=[pl.BlockSpec((tm, d), lambda i, j: (i, 0)),
                  pl.BlockSpec((eb, d), lambda i, j: (j, 0)),
                  pl.BlockSpec((d, eb), lambda i, j: (0, j)),
                  rspec, rspec, rspec, rspec],
        out_specs=pl.BlockSpec((tm, d), lambda i, j: (i, 0)),
        out_shape=jax.ShapeDtypeStruct((n_tok, d), jnp.float32),
        scratch_shapes=[pltpu.VMEM((d, tm), jnp.float32),
                        pltpu.VMEM((eb, tm), u.dtype)],
        compiler_params=_cparams("parallel", "arbitrary"),
        name="peer_ffn",
    )(xb, u, vt, *route)


def _chunk_tri():
    i = np.arange(LANES)
    same = (i[:, None] // HG_CHUNK) == (i[None, :] // HG_CHUNK)
    return jnp.asarray(same & (i[None, :] <= i[:, None]), jnp.bfloat16)


def kernel(x, w_in, hgrn_lb_logits, hgrn_norm_g, s5_lambda_re, s5_lambda_im, s5_log_step,
           s5_b_re, s5_b_im, s5_c_re, s5_c_im, s5_d, s5_w_glu, w_up_a, w_up_b, w_o,
           ln1_g, ln1_b, peer_w_q, peer_keys, peer_u, peer_v, ln2_g, ln2_b):
    bsz, n_tok, d = x.shape
    assert bsz == 1 and d == D_MODEL and n_tok % (S5_CHUNK * 8) == 0
    f32 = jnp.float32
    p = jax.nn.softmax(hgrn_lb_logits.astype(f32), axis=0)
    c = jnp.cumsum(p, axis=0)
    lbs = c - c[0:1]
    loglb = jnp.log(lbs)
    log1mlb = jnp.log1p(-lbs)
    tri = _chunk_tri()
    nc = n_tok // S5_CHUNK

    xf = x.reshape(n_tok, d).astype(f32)
    xb = _bf(xf)
    for l in range(DEPTH):
        proj = _matmul(xb, _bf(w_in[l]), f32, tm=512, tn=1024)
        oa = _hgrn2(proj, loglb[l:l + 1], log1mlb[l:l + 1], hgrn_norm_g[l:l + 1].astype(f32),
                    tri, tb=512)
        tables = _s5_tables(s5_lambda_re[l], s5_lambda_im[l], s5_log_step[l], s5_b_re[l],
                            s5_b_im[l], s5_c_re[l], s5_c_im[l], s5_d[l])
        u = proj[:, 4 * A_WIDTH:4 * A_WIDTH + B_WIDTH]
        ut = jnp.transpose(u.reshape(nc, S5_CHUNK, B_GROUPS, B_GROUP_CH), (2, 0, 1, 3))
        yt = _s5(ut.reshape(B_GROUPS, nc, S5_ROW), tables, cb=64)
        yb = jnp.transpose(yt.reshape(B_GROUPS, nc, S5_CHUNK, B_GROUP_CH), (1, 2, 0, 3))
        ob = _glu(yb.reshape(n_tok, B_WIDTH), _bf(s5_w_glu[l]), tm=1024, tn=512)
        merged = _up_merge(oa, ob, _bf(w_up_a[l]), _bf(w_up_b[l]), proj, tm=1024, tn=512)
        xf, xb = _wo_ln(merged, _bf(w_o[l]), xf, ln1_g[l][None].astype(f32),
                        ln1_b[l][None].astype(f32), tm=256)
        q = _matmul(xb, _bf(peer_w_q[l]), f32, tm=512, tn=1024)
        route = _peer_route(q, peer_keys[l].astype(f32), tt=128)
        y = _peer_ffn(xb, _bf(peer_u[l]), _bf(peer_v[l].T), route, tm=256, eb=512)
        xf, xb = _add_ln(xf, y, ln2_g[l][None].astype(f32), ln2_b[l][None].astype(f32), tm=256)
    return xf.reshape(bsz, n_tok, d).astype(x.dtype)
```

```python
import functools
import math

import jax
import jax.numpy as jnp
import numpy as np
from jax import lax
from jax.experimental import pallas as pl
from jax.experimental.pallas import tpu as pltpu

D_MODEL = 2048
DEPTH = 4
A_HEADS = 8
A_DIM = 128
A_WIDTH = A_HEADS * A_DIM
HG_CHUNK = 16
HG_UNROLL = 8
B_GROUPS = 64
B_GROUP_CH = 16
B_STATE = 64
B_WIDTH = B_GROUPS * B_GROUP_CH
S5_CHUNK = 16
S5_ROW = S5_CHUNK * B_GROUP_CH
S5_EIG_CLIP = -1e-4
IN_COLS = 4 * A_WIDTH + B_WIDTH + 2 * D_MODEL
P_HEADS = 8
P_QDIM = 256
P_HALF = 128
P_NKEYS = 128
P_NEXP = P_NKEYS * P_NKEYS
P_TOPK = 16
ALPHA = (2.0 * DEPTH) ** 0.25
LN_EPS = 1e-5
RMS_EPS = 1e-6

LANES = 128
GATE_COLS = 128
VMEM_LIMIT = 52 * 1024 * 1024

_NT = (((1,), (1,)), ((), ()))
_TN = (((0,), (0,)), ((), ()))


def _cparams(*sem):
    return pltpu.CompilerParams(dimension_semantics=sem, vmem_limit_bytes=VMEM_LIMIT)


def _bf(x):
    return x.astype(jnp.bfloat16)


def _dot(a, b):
    return jnp.dot(a, b, preferred_element_type=jnp.float32)


def _sigmoid(x):
    return 1.0 / (1.0 + jnp.exp(-x))


def _gelu(x):
    c = math.sqrt(2.0 / math.pi)
    return 0.5 * x * (1.0 + jnp.tanh(c * (x + 0.044715 * (x * x * x))))


def _layer_norm(z, g, b):
    mu = jnp.mean(z, axis=-1, keepdims=True)
    zc = z - mu
    var = jnp.mean(zc * zc, axis=-1, keepdims=True)
    return zc * lax.rsqrt(var + LN_EPS) * g + b


def _matmul_kernel(a_ref, w_ref, o_ref):
    o_ref[...] = _dot(a_ref[...], w_ref[...]).astype(o_ref.dtype)


def _matmul(a, w, out_dtype, tm, tn):
    m, k = a.shape
    n = w.shape[1]
    tm = min(tm, m)
    return pl.pallas_call(
        _matmul_kernel,
        grid=(n // tn, m // tm),
        in_specs=[pl.BlockSpec((tm, k), lambda j, i: (i, 0)),
                  pl.BlockSpec((k, tn), lambda j, i: (0, j))],
        out_specs=pl.BlockSpec((tm, tn), lambda j, i: (i, j)),
        out_shape=jax.ShapeDtypeStruct((m, n), out_dtype),
        compiler_params=_cparams("parallel", "parallel"),
        name="matmul",
    )(a, w)


def _hgrn2_kernel(q_ref, f_ref, i_ref, g_ref, loglb_ref, log1mlb_ref, ng_ref, tri_ref,
                  o_ref, st_ref, b_ref, k_ref):
    tb = q_ref.shape[0]

    @pl.when(pl.program_id(1) == 0)
    def _():
        st_ref[...] = jnp.zeros_like(st_ref)

    fz = f_ref[...]
    log_sig = jnp.minimum(fz, 0.0) - jnp.log1p(jnp.exp(-jnp.abs(fz)))
    z = log1mlb_ref[...] + log_sig
    a = loglb_ref[...]
    logf = jnp.maximum(a, z) + jnp.log1p(jnp.exp(-jnp.abs(a - z)))
    k_ref[...] = 1.0 - jnp.exp(logf)
    tri = tri_ref[...]
    for r in range(tb // LANES):
        rows = slice(r * LANES, (r + 1) * LANES)
        lf = logf[rows]
        hi = _bf(lf)
        r1 = lf - hi.astype(jnp.float32)
        mid = _bf(r1)
        lo = _bf(r1 - mid.astype(jnp.float32))
        b_ref[rows, :] = _dot(tri, hi) + _dot(tri, mid) + _dot(tri, lo)

    srow = lax.broadcasted_iota(jnp.int32, (HG_CHUNK, 1), 0)
    ng = ng_ref[...]

    def chunk(j, st):
        rows = pl.ds(pl.multiple_of(j * HG_CHUNK, HG_CHUNK), HG_CHUNK)
        b = b_ref[rows, :]
        q = q_ref[rows, :]
        k = k_ref[rows, :]
        v = i_ref[rows, :]
        b_last = b[HG_CHUNK - 1:HG_CHUNK, :]
        o = lax.dot_general(_bf(q * jnp.exp(b)), _bf(st), _NT,
                            preferred_element_type=jnp.float32)
        intra = []
        for t in range(HG_CHUNK):
            diff = jnp.where(srow <= t, b[t:t + 1, :] - b, -jnp.inf)
            p = (q[t:t + 1, :] * k) * jnp.exp(diff)
            att = jnp.sum(p, axis=-1, keepdims=True)
            intra.append(jnp.sum(att * v, axis=0, keepdims=True))
        o = o + jnp.concatenate(intra, axis=0)
        k_tail = k * jnp.exp(b_last - b)
        st = st * jnp.exp(b_last) + lax.dot_general(
            _bf(v), _bf(k_tail), _TN, preferred_element_type=jnp.float32)
        o = o * lax.rsqrt(jnp.mean(o * o, axis=-1, keepdims=True) + RMS_EPS)
        o = o * ng * _sigmoid(g_ref[rows, :])
        o_ref[rows, :] = o.astype(o_ref.dtype)
        return st

    st_ref[...] = lax.fori_loop(0, tb // HG_CHUNK, chunk, st_ref[...], unroll=HG_UNROLL)


def _hgrn2(proj, loglb, log1mlb, norm_g, tri, tb):
    n_tok = proj.shape[0]
    tb = min(tb, n_tok)

    def col(off):
        return pl.BlockSpec((tb, A_DIM), lambda h, i: (i, off + h))

    def par():
        return pl.BlockSpec((1, A_DIM), lambda h, i: (0, h))

    return pl.pallas_call(
        _hgrn2_kernel,
        grid=(A_HEADS, n_tok // tb),
        in_specs=[col(0), col(A_HEADS), col(2 * A_HEADS), col(3 * A_HEADS),
                  par(), par(), par(),
                  pl.BlockSpec((LANES, LANES), lambda h, i: (0, 0))],
        out_specs=pl.BlockSpec((tb, A_DIM), lambda h, i: (i, h)),
        out_shape=jax.ShapeDtypeStruct((n_tok, A_WIDTH), jnp.bfloat16),
        scratch_shapes=[pltpu.VMEM((A_DIM, A_DIM), jnp.float32),
                        pltpu.VMEM((tb, A_DIM), jnp.float32),
                        pltpu.VMEM((tb, A_DIM), jnp.float32)],
        compiler_params=_cparams("parallel", "arbitrary"),
        name="hgrn2",
    )(proj, proj, proj, proj, loglb, log1mlb, norm_g, tri)


def _s5_tables(lam_re, lam_im, log_step, b_re, b_im, c_re, c_im, d):
    f32 = jnp.float32
    hp = lax.Precision.HIGHEST
    t = S5_CHUNK
    lr = jnp.minimum(lam_re.astype(f32), S5_EIG_CLIP)
    li = lam_im.astype(f32)
    dt = jnp.exp(log_step.astype(f32))[:, None]
    mag = jnp.exp(lr * dt)
    ar = mag * jnp.cos(li * dt)
    ai = mag * jnp.sin(li * dt)
    den = lr * lr + li * li
    nr = ar - 1.0
    zr = (nr * lr + ai * li) / den
    zi = (ai * lr - nr * li) / den
    br_, bi_ = b_re.astype(f32), b_im.astype(f32)
    bbr = zr[..., None] * br_ - zi[..., None] * bi_
    bbi = zr[..., None] * bi_ + zi[..., None] * br_
    pr = [jnp.ones_like(ar)]
    pi = [jnp.zeros_like(ai)]
    for _ in range(t):
        r0, i0 = pr[-1], pi[-1]
        pr.append(r0 * ar - i0 * ai)
        pi.append(r0 * ai + i0 * ar)
    pwr = jnp.stack(pr, axis=1)
    pwi = jnp.stack(pi, axis=1)
    cr, ci = c_re.astype(f32), c_im.astype(f32)
    car = cr[:, None] * pwr[:, :t, None, :] - ci[:, None] * pwi[:, :t, None, :]
    cai = cr[:, None] * pwi[:, :t, None, :] + ci[:, None] * pwr[:, :t, None, :]
    kk = (jnp.einsum('gtnp,gpm->gtnm', car, bbr, precision=hp)
          - jnp.einsum('gtnp,gpm->gtnm', cai, bbi, precision=hp))
    ti = np.arange(t)
    lag = ti[None, :] - ti[:, None]
    toep = kk[:, np.clip(lag, 0, t - 1)]
    toep = jnp.where((lag >= 0)[None, :, :, None, None], toep, 0.0)
    toep = jnp.transpose(toep, (0, 1, 4, 2, 3)).reshape(B_GROUPS, S5_ROW, S5_ROW)
    er = pwr[:, t - 1 - ti][:, :, None, :]
    ei = pwi[:, t - 1 - ti][:, :, None, :]
    bbr_t = jnp.transpose(bbr, (0, 2, 1))[:, None]
    bbi_t = jnp.transpose(bbi, (0, 2, 1))[:, None]
    w1 = jnp.concatenate([er * bbr_t - ei * bbi_t, er * bbi_t + ei * bbr_t], axis=-1)
    w1 = w1.reshape(B_GROUPS, S5_ROW, 2 * B_STATE)
    zr1 = pwr[:, 1:, None, :]
    zi1 = pwi[:, 1:, None, :]
    w2r = cr[:, None] * zr1 - ci[:, None] * zi1
    w2i = -(cr[:, None] * zi1 + ci[:, None] * zr1)
    w2 = jnp.concatenate([w2r, w2i], axis=-1)
    w2 = jnp.transpose(w2, (0, 3, 1, 2)).reshape(B_GROUPS, 2 * B_STATE, S5_ROW)
    a1 = jnp.concatenate([pwr[:, t], pwr[:, t]], axis=-1)
    a2 = jnp.concatenate([-pwi[:, t], pwi[:, t]], axis=-1)
    dd = jnp.tile(d.astype(f32), (1, t))[:, None, :]
    return _bf(toep), _bf(w1), _bf(w2), a1, a2, dd


def _s5_local_kernel(u_ref, w1_ref, e_ref):
    e_ref[...] = _dot(_bf(u_ref[0]), w1_ref[0])


def _s5_scan_kernel(e_ref, a1_ref, a2_ref, xp_ref, st_ref):
    @pl.when(pl.program_id(0) == 0)
    def _():
        st_ref[...] = jnp.zeros_like(st_ref)

    a1 = a1_ref[...]
    a2 = a2_ref[...]

    def step(c, carry):
        x = st_ref[...]
        xp_ref[c] = x
        st_ref[...] = a1 * x + a2 * pltpu.roll(x, B_STATE, 1) + e_ref[c]
        return carry

    lax.fori_loop(0, e_ref.shape[0], step, 0)


def _s5_out_kernel(u_ref, xp_ref, toep_ref, w2_ref, d_ref, y_ref):
    u = u_ref[0]
    y = (_dot(_bf(u), toep_ref[0]) + _dot(_bf(xp_ref[...]), w2_ref[0]) + d_ref[0] * u)
    y_ref[0] = _gelu(y).astype(y_ref.dtype)


def _s5(ut, tables, cb):
    toep, w1, w2, a1, a2, dd = tables
    g, nc, _ = ut.shape
    ns = 2 * B_STATE
    cb = min(cb, nc)
    e = pl.pallas_call(
        _s5_local_kernel,
        grid=(g,),
        in_specs=[pl.BlockSpec((1, nc, S5_ROW), lambda i: (i, 0, 0)),
                  pl.BlockSpec((1, S5_ROW, ns), lambda i: (i, 0, 0))],
        out_specs=pl.BlockSpec((nc, ns), lambda i: (0, i)),
        out_shape=jax.ShapeDtypeStruct((nc, g * ns), jnp.float32),
        compiler_params=_cparams("parallel"),
        name="s5_local",
    )(ut, w1)
    xp = pl.pallas_call(
        _s5_scan_kernel,
        grid=(nc // cb,),
        in_specs=[pl.BlockSpec((cb, g, ns), lambda i: (i, 0, 0)),
                  pl.BlockSpec((g, ns), lambda i: (0, 0)),
                  pl.BlockSpec((g, ns), lambda i: (0, 0))],
        out_specs=pl.BlockSpec((cb, g, ns), lambda i: (i, 0, 0)),
        out_shape=jax.ShapeDtypeStruct((nc, g, ns), jnp.float32),
        scratch_shapes=[pltpu.VMEM((g, ns), jnp.float32)],
        compiler_params=_cparams("arbitrary"),
        name="s5_scan",
    )(e.reshape(nc, g, ns), a1, a2)
    return pl.pallas_call(
        _s5_out_kernel,
        grid=(g,),
        in_specs=[pl.BlockSpec((1, nc, S5_ROW), lambda i: (i, 0, 0)),
                  pl.BlockSpec((nc, ns), lambda i: (0, i)),
                  pl.BlockSpec((1, S5_ROW, S5_ROW), lambda i: (i, 0, 0)),
                  pl.BlockSpec((1, ns, S5_ROW), lambda i: (i, 0, 0)),
                  pl.BlockSpec((1, 1, S5_ROW), lambda i: (i, 0, 0))],
        out_specs=pl.BlockSpec((1, nc, S5_ROW), lambda i: (i, 0, 0)),
        out_shape=jax.ShapeDtypeStruct((g, nc, S5_ROW), jnp.bfloat16),
        compiler_params=_cparams("parallel"),
        name="s5_out",
    )(ut, xp.reshape(nc, g * ns), toep, w2, dd)


def _glu_kernel(y_ref, wa_ref, wb_ref, o_ref):
    y = y_ref[...]
    o_ref[...] = (_dot(y, wa_ref[...]) * _sigmoid(_dot(y, wb_ref[...]))).astype(o_ref.dtype)


def _glu(y, w, tm, tn):
    m, k = y.shape
    n = w.shape[1] // 2
    tm = min(tm, m)
    nb = n // tn
    return pl.pallas_call(
        _glu_kernel,
        grid=(nb, m // tm),
        in_specs=[pl.BlockSpec((tm, k), lambda j, i: (i, 0)),
                  pl.BlockSpec((k, tn), lambda j, i: (0, j)),
                  pl.BlockSpec((k, tn), lambda j, i: (0, nb + j))],
        out_specs=pl.BlockSpec((tm, tn), lambda j, i: (i, j)),
        out_shape=jax.ShapeDtypeStruct((m, n), jnp.bfloat16),
        compiler_params=_cparams("parallel", "parallel"),
        name="glu",
    )(y, w, w)


def _up_merge_kernel(oa_ref, ob_ref, wa_ref, wb_ref, ga_ref, gb_ref, o_ref):
    m = (_sigmoid(ga_ref[...]) * _dot(oa_ref[...], wa_ref[...])
         + _sigmoid(gb_ref[...]) * _dot(ob_ref[...], wb_ref[...]))
    o_ref[...] = m.astype(o_ref.dtype)


def _up_merge(oa, ob, wa, wb, proj, tm, tn):
    m, k = oa.shape
    n = wa.shape[1]
    tm = min(tm, m)
    ga0 = (4 * A_WIDTH + B_WIDTH) // tn
    gb0 = ga0 + D_MODEL // tn
    return pl.pallas_call(
        _up_merge_kernel,
        grid=(n // tn, m // tm),
        in_specs=[pl.BlockSpec((tm, k), lambda j, i: (i, 0)),
                  pl.BlockSpec((tm, k), lambda j, i: (i, 0)),
                  pl.BlockSpec((k, tn), lambda j, i: (0, j)),
                  pl.BlockSpec((k, tn), lambda j, i: (0, j)),
                  pl.BlockSpec((tm, tn), lambda j, i: (i, ga0 + j)),
                  pl.BlockSpec((tm, tn), lambda j, i: (i, gb0 + j))],
        out_specs=pl.BlockSpec((tm, tn), lambda j, i: (i, j)),
        out_shape=jax.ShapeDtypeStruct((m, n), jnp.bfloat16),
        compiler_params=_cparams("parallel", "parallel"),
        name="up_merge",
    )(oa, ob, wa, wb, proj, proj)


def _wo_ln_kernel(m_ref, w_ref, x_ref, g_ref, b_ref, o_ref, ob_ref):
    z = ALPHA * x_ref[...] + _dot(m_ref[...], w_ref[...])
    y = _layer_norm(z, g_ref[...], b_ref[...])
    o_ref[...] = y
    ob_ref[...] = _bf(y)


def _wo_ln(merged, w, x, g, b, tm):
    m, k = merged.shape
    n = w.shape[1]
    tm = min(tm, m)
    row = lambda i: (i, 0)
    fix = lambda i: (0, 0)
    return pl.pallas_call(
        _wo_ln_kernel,
        grid=(m // tm,),
        in_specs=[pl.BlockSpec((tm, k), row), pl.BlockSpec((k, n), fix),
                  pl.BlockSpec((tm, n), row), pl.BlockSpec((1, n), fix),
                  pl.BlockSpec((1, n), fix)],
        out_specs=[pl.BlockSpec((tm, n), row), pl.BlockSpec((tm, n), row)],
        out_shape=[jax.ShapeDtypeStruct((m, n), jnp.float32),
                   jax.ShapeDtypeStruct((m, n), jnp.bfloat16)],
        compiler_params=_cparams("parallel"),
        name="wo_ln",
    )(merged, w, x, g, b)


def _add_ln_kernel(x_ref, y_ref, g_ref, b_ref, o_ref, ob_ref):
    y = _layer_norm(ALPHA * x_ref[...] + y_ref[...], g_ref[...], b_ref[...])
    o_ref[...] = y
    ob_ref[...] = _bf(y)


def _add_ln(x, y, g, b, tm):
    m, n = x.shape
    tm = min(tm, m)
    row = lambda i: (i, 0)
    fix = lambda i: (0, 0)
    return pl.pallas_call(
        _add_ln_kernel,
        grid=(m // tm,),
        in_specs=[pl.BlockSpec((tm, n), row), pl.BlockSpec((tm, n), row),
                  pl.BlockSpec((1, n), fix), pl.BlockSpec((1, n), fix)],
        out_specs=[pl.BlockSpec((tm, n), row), pl.BlockSpec((tm, n), row)],
        out_shape=[jax.ShapeDtypeStruct((m, n), jnp.float32),
                   jax.ShapeDtypeStruct((m, n), jnp.bfloat16)],
        compiler_params=_cparams("parallel"),
        name="add_ln",
    )(x, y, g, b)


def _take_top(s, n_take, on_take):
    rows = s.shape[0]
    rid = lax.broadcasted_iota(jnp.int32, s.shape, 0)
    for k in range(n_take):
        m = jnp.max(s, axis=0, keepdims=True)
        idx = jnp.min(jnp.where(s == m, rid, rows), axis=0, keepdims=True)
        on_take(k, m, idx)
        s = jnp.where(rid == idx, -jnp.inf, s)


def _peer_route_kernel(q_ref, keys_ref, r2_ref, e2_ref, n_ref, c_ref):
    tt = q_ref.shape[0]
    kid = lax.broadcasted_iota(jnp.int32, (P_NKEYS, tt), 0)
    rank_id = lax.broadcasted_iota(jnp.int32, (P_TOPK, tt), 0)
    for h in range(P_HEADS):
        s = []
        for half in range(2):
            qh = q_ref[:, (2 * h + half) * P_HALF:(2 * h + half + 1) * P_HALF]
            s.append(lax.dot_general(_bf(keys_ref[h, half]), _bf(qh), _NT,
                                     preferred_element_type=jnp.float32))
        vals = [[], []]
        idxs = [[], []]
        rank2 = [jnp.full((P_NKEYS, tt), P_TOPK, jnp.int32)]
        for half in range(2):
            def take(k, m, idx, half=half):
                vals[half].append(m)
                idxs[half].append(idx)
                if half == 1:
                    rank2[0] = jnp.where(kid == idx, k, rank2[0])
            _take_top(s[half], P_TOPK, take)
        v1 = jnp.concatenate(vals[0], axis=0)
        v2 = jnp.concatenate(vals[1], axis=0)
        widths = [P_TOPK // (i + 1) for i in range(P_TOPK)]
        starts = np.cumsum([0] + widths)
        pad = int(-starts[-1] % 8)
        cand = jnp.concatenate([v1[i:i + 1] + v2[:widths[i]] for i in range(P_TOPK)]
                               + [jnp.full((pad, tt), -jnp.inf, jnp.float32)], axis=0)
        state = [jnp.zeros((P_TOPK, tt), jnp.int32), jnp.zeros((1, tt), jnp.float32)]
        top = v1[0:1] + v2[0:1]

        def take_c(k, m, pos):
            rank1 = sum((pos >= int(st)).astype(jnp.int32) for st in starts[1:P_TOPK])
            state[0] = state[0] + (rank_id == rank1).astype(jnp.int32)
            state[1] = state[1] + jnp.exp(m - top)
        _take_top(cand, P_TOPK, take_c)
        n_rank, z = state
        n_key = jnp.zeros((P_NKEYS, tt), jnp.int32)
        for i in range(P_TOPK):
            n_key = jnp.where(kid == idxs[0][i], n_rank[i:i + 1], n_key)
        r2_ref[h] = rank2[0].astype(jnp.float32).astype(r2_ref.dtype)
        n_ref[h] = n_key.astype(jnp.float32)
        e2_ref[h] = jnp.exp(s[1] - v2[0:1]).astype(e2_ref.dtype)
        c_ref[h] = jnp.exp(s[0] - v1[0:1]) / z


def _peer_route(q, keys, tt, gate_dtype=jnp.float32):
    n_tok = q.shape[0]
    tt = min(tt, n_tok)
    out = [jax.ShapeDtypeStruct((P_HEADS, P_NKEYS, n_tok), dt)
           for dt in (gate_dtype, gate_dtype, jnp.float32, jnp.float32)]
    ospec = pl.BlockSpec((P_HEADS, P_NKEYS, tt), lambda i: (0, 0, i))
    return pl.pallas_call(
        _peer_route_kernel,
        grid=(n_tok // tt,),
        in_specs=[pl.BlockSpec((tt, P_HEADS * P_QDIM), lambda i: (i, 0)),
                  pl.BlockSpec((P_HEADS, 2, P_NKEYS, P_HALF), lambda i: (0, 0, 0, 0))],
        out_specs=[ospec] * 4,
        out_shape=out,
        compiler_params=_cparams("parallel"),
        name="peer_route",
    )(q, keys)


def _peer_ffn_kernel(x_ref, u_ref, vt_ref, r2_ref, e2_ref, n_ref, c_ref, y_ref, acc_ref, h_ref):
    j = pl.program_id(1)
    eb = u_ref.shape[0]
    tm = x_ref.shape[0]

    @pl.when(j == 0)
    def _():
        acc_ref[...] = jnp.zeros_like(acc_ref)

    act = lax.dot_general(u_ref[...], x_ref[...], _NT, preferred_element_type=jnp.float32)
    for al in range(eb // P_NKEYS):
        a = j * (eb // P_NKEYS) + al
        rows = slice(al * P_NKEYS, (al + 1) * P_NKEYS)
        for cc in range(tm // GATE_COLS):
            cols = slice(cc * GATE_COLS, (cc + 1) * GATE_COLS)
            gate = None
            for h in range(P_HEADS):
                n_row = n_ref[h, pl.ds(a, 1), :][:, cols]
                c_row = c_ref[h, pl.ds(a, 1), :][:, cols]
                term = jnp.where(r2_ref[h, :, cols] < n_row, e2_ref[h, :, cols] * c_row, 0.0)
                gate = term if gate is None else gate + term
            h_ref[rows, cols] = (_gelu(act[rows, cols]) * gate).astype(h_ref.dtype)
    acc_ref[...] += _dot(vt_ref[...], h_ref[...])

    @pl.when(j == pl.num_programs(1) - 1)
    def _():
        y_ref[...] = acc_ref[...].T


def _peer_ffn(xb, u, vt, route, tm, eb):
    n_tok, d = xb.shape
    tm = min(tm, n_tok)
    rspec = pl.BlockSpec((P_HEADS, P_NKEYS, tm), lambda i, j: (0, 0, i))
    return pl.pallas_call(
        _peer_ffn_kernel,
        grid=(n_tok // tm, P_NEXP // eb),
        in_specs=[pl.BlockSpec((tm, d), lambda i, j: (i, 0)),
                  pl.BlockSpec((eb, d), lambda i, j: (j, 0)),
                  pl.BlockSpec((d, eb), lambda i, j: (0, j)),
                  rspec, rspec, rspec, rspec],
        out_specs=pl.BlockSpec((tm, d), lambda i, j: (i, 0)),
        out_shape=jax.ShapeDtypeStruct((n_tok, d), jnp.float32),
        scratch_shapes=[pltpu.VMEM((d, tm), jnp.float32),
                        pltpu.VMEM((eb, tm), u.dtype)],
        compiler_params=_cparams("parallel", "arbitrary"),
        name="peer_ffn",
    )(xb, u, vt, *route)


def _chunk_tri():
    i = np.arange(LANES)
    same = (i[:, None] // HG_CHUNK) == (i[None, :] // HG_CHUNK)
    return jnp.asarray(same & (i[None, :] <= i[:, None]), jnp.bfloat16)


def kernel(x, w_in, hgrn_lb_logits, hgrn_norm_g, s5_lambda_re, s5_lambda_im, s5_log_step,
           s5_b_re, s5_b_im, s5_c_re, s5_c_im, s5_d, s5_w_glu, w_up_a, w_up_b, w_o,
           ln1_g, ln1_b, peer_w_q, peer_keys, peer_u, peer_v, ln2_g, ln2_b):
    bsz, n_tok, d = x.shape
    assert bsz == 1 and d == D_MODEL and n_tok % (S5_CHUNK * 8) == 0
    f32 = jnp.float32
    p = jax.nn.softmax(hgrn_lb_logits.astype(f32), axis=0)
    c = jnp.cumsum(p, axis=0)
    lbs = c - c[0:1]
    loglb = jnp.log(lbs)
    log1mlb = jnp.log1p(-lbs)
    tri = _chunk_tri()
    nc = n_tok // S5_CHUNK

    xf = x.reshape(n_tok, d).astype(f32)
    xb = _bf(xf)
    for l in range(DEPTH):
        proj = _matmul(xb, _bf(w_in[l]), f32, tm=512, tn=1024)
        oa = _hgrn2(proj, loglb[l:l + 1], log1mlb[l:l + 1], hgrn_norm_g[l:l + 1].astype(f32),
                    tri, tb=512)
        tables = _s5_tables(s5_lambda_re[l], s5_lambda_im[l], s5_log_step[l], s5_b_re[l],
                            s5_b_im[l], s5_c_re[l], s5_c_im[l], s5_d[l])
        u = proj[:, 4 * A_WIDTH:4 * A_WIDTH + B_WIDTH]
        ut = jnp.transpose(u.reshape(nc, S5_CHUNK, B_GROUPS, B_GROUP_CH), (2, 0, 1, 3))
        yt = _s5(ut.reshape(B_GROUPS, nc, S5_ROW), tables, cb=64)
        yb = jnp.transpose(yt.reshape(B_GROUPS, nc, S5_CHUNK, B_GROUP_CH), (1, 2, 0, 3))
        ob = _glu(yb.reshape(n_tok, B_WIDTH), _bf(s5_w_glu[l]), tm=1024, tn=512)
        merged = _up_merge(oa, ob, _bf(w_up_a[l]), _bf(w_up_b[l]), proj, tm=1024, tn=512)
        xf, xb = _wo_ln(merged, _bf(w_o[l]), xf, ln1_g[l][None].astype(f32),
                        ln1_b[l][None].astype(f32), tm=256)
        q = _matmul(xb, _bf(peer_w_q[l]), f32, tm=512, tn=1024)
        route = _peer_route(q, peer_keys[l].astype(f32), tt=128)
        y = _peer_ffn(xb, _bf(peer_u[l]), _bf(peer_v[l].T), route, tm=512, eb=512)
        xf, xb = _add_ln(xf, y, ln2_g[l][None].astype(f32), ln2_b[l][None].astype(f32), tm=256)
    return xf.reshape(bsz, n_tok, d).astype(x.dtype)
```

```python
import math

import jax
import jax.numpy as jnp
import numpy as np
from jax import lax
from jax.experimental import pallas as pl
from jax.experimental.pallas import tpu as pltpu

D_MODEL = 2048
DEPTH = 4
A_HEADS = 8
A_DIM = 128
A_WIDTH = A_HEADS * A_DIM
HG_CHUNK = 16
HG_UNROLL = 8
B_GROUPS = 64
B_GROUP_CH = 16
B_STATE = 64
B_WIDTH = B_GROUPS * B_GROUP_CH
S5_CHUNK = 16
S5_SLAB_GROUPS = 8
S5_EIG_CLIP = -1e-4
IN_COLS = 4 * A_WIDTH + B_WIDTH + 2 * D_MODEL
P_HEADS = 8
P_QDIM = 256
P_HALF = 128
P_NKEYS = 128
P_NEXP = P_NKEYS * P_NKEYS
P_TOPK = 16
ALPHA = (2.0 * DEPTH) ** 0.25
LN_EPS = 1e-5
RMS_EPS = 1e-6

LANES = 128
GATE_COLS = 128
VMEM_LIMIT = 52 * 1024 * 1024

_NT = (((1,), (1,)), ((), ()))
_TN = (((0,), (0,)), ((), ()))


def _cparams(*sem):
    return pltpu.CompilerParams(dimension_semantics=sem, vmem_limit_bytes=VMEM_LIMIT)


def _bf(x):
    return x.astype(jnp.bfloat16)


def _dot(a, b):
    return jnp.dot(a, b, preferred_element_type=jnp.float32)


def _sigmoid(x):
    return 1.0 / (1.0 + jnp.exp(-x))


def _gelu(x):
    c = math.sqrt(2.0 / math.pi)
    return 0.5 * x * (1.0 + jnp.tanh(c * (x + 0.044715 * (x * x * x))))


def _layer_norm(z, g, b):
    mu = jnp.mean(z, axis=-1, keepdims=True)
    zc = z - mu
    var = jnp.mean(zc * zc, axis=-1, keepdims=True)
    return zc * lax.rsqrt(var + LN_EPS) * g + b


def _matmul_kernel(a_ref, w_ref, o_ref):
    o_ref[...] = _dot(a_ref[...], w_ref[...]).astype(o_ref.dtype)


def _matmul(a, w, layer, out_dtype, tm, tn):
    m, k = a.shape
    n = w.shape[2]
    tm = min(tm, m)
    return pl.pallas_call(
        _matmul_kernel,
        grid=(n // tn, m // tm),
        in_specs=[pl.BlockSpec((tm, k), lambda j, i: (i, 0)),
                  pl.BlockSpec((None, k, tn), lambda j, i: (layer, 0, j))],
        out_specs=pl.BlockSpec((tm, tn), lambda j, i: (i, j)),
        out_shape=jax.ShapeDtypeStruct((m, n), out_dtype),
        compiler_params=_cparams("parallel", "parallel"),
        name="matmul",
    )(a, w)


def _hgrn2_kernel(q_ref, f_ref, i_ref, g_ref, loglb_ref, log1mlb_ref, ng_ref, tri_ref,
                  o_ref, st_ref, b_ref, k_ref):
    tb = q_ref.shape[0]

    @pl.when(pl.program_id(1) == 0)
    def _():
        st_ref[...] = jnp.zeros_like(st_ref)

    fz = f_ref[...]
    log_sig = jnp.minimum(fz, 0.0) - jnp.log1p(jnp.exp(-jnp.abs(fz)))
    z = log1mlb_ref[...] + log_sig
    a = loglb_ref[...]
    logf = jnp.maximum(a, z) + jnp.log1p(jnp.exp(-jnp.abs(a - z)))
    k_ref[...] = 1.0 - jnp.exp(logf)
    tri = tri_ref[...]
    for r in range(tb // LANES):
        rows = slice(r * LANES, (r + 1) * LANES)
        lf = logf[rows]
        hi = _bf(lf)
        r1 = lf - hi.astype(jnp.float32)
        mid = _bf(r1)
        lo = _bf(r1 - mid.astype(jnp.float32))
        b_ref[rows, :] = _dot(tri, hi) + _dot(tri, mid) + _dot(tri, lo)

    srow = lax.broadcasted_iota(jnp.int32, (HG_CHUNK, 1), 0)
    ng = ng_ref[...]

    def chunk(j, st):
        rows = pl.ds(pl.multiple_of(j * HG_CHUNK, HG_CHUNK), HG_CHUNK)
        b = b_ref[rows, :]
        q = q_ref[rows, :]
        k = k_ref[rows, :]
        v = i_ref[rows, :]
        b_last = b[HG_CHUNK - 1:HG_CHUNK, :]
        o = lax.dot_general(_bf(q * jnp.exp(b)), _bf(st), _NT,
                            preferred_element_type=jnp.float32)
        intra = []
        for t in range(HG_CHUNK):
            diff = jnp.where(srow <= t, b[t:t + 1, :] - b, -jnp.inf)
            p = (q[t:t + 1, :] * k) * jnp.exp(diff)
            att = jnp.sum(p, axis=-1, keepdims=True)
            intra.append(jnp.sum(att * v, axis=0, keepdims=True))
        o = o + jnp.concatenate(intra, axis=0)
        k_tail = k * jnp.exp(b_last - b)
        st = st * jnp.exp(b_last) + lax.dot_general(
            _bf(v), _bf(k_tail), _TN, preferred_element_type=jnp.float32)
        o = o * lax.rsqrt(jnp.mean(o * o, axis=-1, keepdims=True) + RMS_EPS)
        o = o * ng * _sigmoid(g_ref[rows, :])
        o_ref[rows, :] = o.astype(o_ref.dtype)
        return st

    st_ref[...] = lax.fori_loop(0, tb // HG_CHUNK, chunk, st_ref[...], unroll=HG_UNROLL)


def _hgrn2(proj, loglb, log1mlb, norm_g, tri, tb):
    n_tok = proj.shape[0]
    tb = min(tb, n_tok)

    def col(off):
        return pl.BlockSpec((tb, A_DIM), lambda h, i: (i, off + h))

    def par():
        return pl.BlockSpec((1, A_DIM), lambda h, i: (0, h))

    return pl.pallas_call(
        _hgrn2_kernel,
        grid=(A_HEADS, n_tok // tb),
        in_specs=[col(0), col(A_HEADS), col(2 * A_HEADS), col(3 * A_HEADS),
                  par(), par(), par(),
                  pl.BlockSpec((LANES, LANES), lambda h, i: (0, 0))],
        out_specs=pl.BlockSpec((tb, A_DIM), lambda h, i: (i, h)),
        out_shape=jax.ShapeDtypeStruct((n_tok, A_WIDTH), jnp.bfloat16),
        scratch_shapes=[pltpu.VMEM((A_DIM, A_DIM), jnp.float32),
                        pltpu.VMEM((tb, A_DIM), jnp.float32),
                        pltpu.VMEM((tb, A_DIM), jnp.float32)],
        compiler_params=_cparams("parallel", "arbitrary"),
        name="hgrn2",
    )(proj, proj, proj, proj, loglb, log1mlb, norm_g, tri)


def _s5_tables(lam_re, lam_im, log_step, b_re, b_im, c_re, c_im, d):
    f32 = jnp.float32
    hp = lax.Precision.HIGHEST
    t = S5_CHUNK
    lr = jnp.minimum(lam_re.astype(f32), S5_EIG_CLIP)
    li = lam_im.astype(f32)
    dt = jnp.exp(log_step.astype(f32))[:, None]
    mag = jnp.exp(lr * dt)
    ar = mag * jnp.cos(li * dt)
    ai = mag * jnp.sin(li * dt)
    den = lr * lr + li * li
    nr = ar - 1.0
    zr = (nr * lr + ai * li) / den
    zi = (ai * lr - nr * li) / den
    br_, bi_ = b_re.astype(f32), b_im.astype(f32)
    bbr = zr[..., None] * br_ - zi[..., None] * bi_
    bbi = zr[..., None] * bi_ + zi[..., None] * br_
    pr = [jnp.ones_like(ar)]
    pi = [jnp.zeros_like(ai)]
    for _ in range(t):
        r0, i0 = pr[-1], pi[-1]
        pr.append(r0 * ar - i0 * ai)
        pi.append(r0 * ai + i0 * ar)
    pwr = jnp.stack(pr, axis=1)
    pwi = jnp.stack(pi, axis=1)
    cr, ci = c_re.astype(f32), c_im.astype(f32)
    car = cr[:, None] * pwr[:, :t, None, :] - ci[:, None] * pwi[:, :t, None, :]
    cai = cr[:, None] * pwi[:, :t, None, :] + ci[:, None] * pwr[:, :t, None, :]
    kk = (jnp.einsum('gtnp,gpm->gtnm', car, bbr, precision=hp)
          - jnp.einsum('gtnp,gpm->gtnm', cai, bbi, precision=hp))
    ti = np.arange(t)
    lag = ti[None, :] - ti[:, None]
    toep = kk[:, np.clip(lag, 0, t - 1)]
    toep = jnp.where((lag >= 0)[None, :, :, None, None], toep, 0.0)
    sg = S5_SLAB_GROUPS
    ns = B_GROUPS // sg
    eye = jnp.eye(sg, dtype=f32)
    toep = jnp.einsum('sgtunm,gh->stgmuhn', toep.reshape(ns, sg, t, t, B_GROUP_CH, B_GROUP_CH), eye)
    toep = toep.reshape(ns, t * LANES, t * LANES)
    er = pwr[:, t - 1 - ti][:, :, None, :]
    ei = pwi[:, t - 1 - ti][:, :, None, :]
    bbr_t = jnp.transpose(bbr, (0, 2, 1))[:, None]
    bbi_t = jnp.transpose(bbi, (0, 2, 1))[:, None]
    w1 = jnp.concatenate([er * bbr_t - ei * bbi_t, er * bbi_t + ei * bbr_t], axis=-1)
    w1 = jnp.einsum('sgtmp,gh->stgmhp', w1.reshape(ns, sg, t, B_GROUP_CH, 2 * B_STATE), eye)
    w1 = w1.reshape(ns, t * LANES, sg * 2 * B_STATE)
    zr1 = pwr[:, 1:, None, :]
    zi1 = pwi[:, 1:, None, :]
    w2r = cr[:, None] * zr1 - ci[:, None] * zi1
    w2i = -(cr[:, None] * zi1 + ci[:, None] * zr1)
    w2 = jnp.concatenate([w2r, w2i], axis=-1)
    w2 = jnp.einsum('sgtnp,gh->sgpthn', w2.reshape(ns, sg, t, B_GROUP_CH, 2 * B_STATE), eye)
    w2 = w2.reshape(ns, sg * 2 * B_STATE, t * LANES)
    a1 = jnp.concatenate([pwr[:, t], pwr[:, t]], axis=-1)
    a2 = jnp.concatenate([-pwi[:, t], pwi[:, t]], axis=-1)
    dd = jnp.tile(d.astype(f32).reshape(ns, 1, LANES), (1, 1, t))
    return _bf(toep), _bf(w1), _bf(w2), a1, a2, dd


def _s5_rows(u_ref, nc):
    return jnp.concatenate([u_ref[pl.ds(t, nc, stride=S5_CHUNK), :] for t in range(S5_CHUNK)],
                           axis=1)


def _s5_local_kernel(u_ref, w1_ref, e_ref):
    e_ref[...] = _dot(_bf(_s5_rows(u_ref, e_ref.shape[0])), w1_ref[0])


def _s5_scan_kernel(e_ref, a1_ref, a2_ref, xp_ref, st_ref):
    @pl.when(pl.program_id(0) == 0)
    def _():
        st_ref[...] = jnp.zeros_like(st_ref)

    a1 = a1_ref[...]
    a2 = a2_ref[...]

    def step(c, carry):
        x = st_ref[...]
        xp_ref[c] = x
        st_ref[...] = a1 * x + a2 * pltpu.roll(x, B_STATE, 1) + e_ref[c]
        return carry

    lax.fori_loop(0, e_ref.shape[0], step, 0)


def _s5_out_kernel(u_ref, xp_ref, toep_ref, w2_ref, d_ref, y_ref):
    nc = xp_ref.shape[0]
    x = _s5_rows(u_ref, nc)
    y = _dot(_bf(x), toep_ref[0]) + _dot(_bf(xp_ref[...]), w2_ref[0]) + d_ref[0] * x
    y = _gelu(y)
    for t in range(S5_CHUNK):
        y_ref[pl.ds(t, nc, stride=S5_CHUNK), :] = y[:, t * LANES:(t + 1) * LANES]


def _s5(proj, tables, cb, nrb):
    toep, w1, w2, a1, a2, dd = tables
    n_tok = proj.shape[0]
    nc = n_tok // S5_CHUNK
    g = B_GROUPS
    ns = 2 * B_STATE
    n_slab = B_WIDTH // LANES
    sw = S5_SLAB_GROUPS * ns
    u0 = 4 * A_WIDTH // LANES
    cb = min(cb, nc)
    ncb = nc // nrb
    tokb = n_tok // nrb
    e = pl.pallas_call(
        _s5_local_kernel,
        grid=(n_slab, nrb),
        in_specs=[pl.BlockSpec((tokb, LANES), lambda s, r: (r, u0 + s)),
                  pl.BlockSpec((1, S5_CHUNK * LANES, sw), lambda s, r: (s, 0, 0))],
        out_specs=pl.BlockSpec((ncb, sw), lambda s, r: (r, s)),
        out_shape=jax.ShapeDtypeStruct((nc, g * ns), jnp.float32),
        compiler_params=_cparams("parallel", "parallel"),
        name="s5_local",
    )(proj, w1)
    xp = pl.pallas_call(
        _s5_scan_kernel,
        grid=(nc // cb,),
        in_specs=[pl.BlockSpec((cb, g, ns), lambda i: (i, 0, 0)),
                  pl.BlockSpec((g, ns), lambda i: (0, 0)),
                  pl.BlockSpec((g, ns), lambda i: (0, 0))],
        out_specs=pl.BlockSpec((cb, g, ns), lambda i: (i, 0, 0)),
        out_shape=jax.ShapeDtypeStruct((nc, g, ns), jnp.float32),
        scratch_shapes=[pltpu.VMEM((g, ns), jnp.float32)],
        compiler_params=_cparams("arbitrary"),
        name="s5_scan",
    )(e.reshape(nc, g, ns), a1, a2)
    return pl.pallas_call(
        _s5_out_kernel,
        grid=(n_slab, nrb),
        in_specs=[pl.BlockSpec((tokb, LANES), lambda s, r: (r, u0 + s)),
                  pl.BlockSpec((ncb, sw), lambda s, r: (r, s)),
                  pl.BlockSpec((1, S5_CHUNK * LANES, S5_CHUNK * LANES), lambda s, r: (s, 0, 0)),
                  pl.BlockSpec((1, sw, S5_CHUNK * LANES), lambda s, r: (s, 0, 0)),
                  pl.BlockSpec((1, 1, S5_CHUNK * LANES), lambda s, r: (s, 0, 0))],
        out_specs=pl.BlockSpec((tokb, LANES), lambda s, r: (r, s)),
        out_shape=jax.ShapeDtypeStruct((n_tok, B_WIDTH), jnp.float32),
        compiler_params=_cparams("parallel", "parallel"),
        name="s5_out",
    )(proj, xp.reshape(nc, g * ns), toep, w2, dd)


def _glu_kernel(y_ref, wa_ref, wb_ref, o_ref):
    y = _bf(y_ref[...])
    o_ref[...] = (_dot(y, wa_ref[...]) * _sigmoid(_dot(y, wb_ref[...]))).astype(o_ref.dtype)


def _glu(y, w, layer, tm, tn):
    m, k = y.shape
    n = w.shape[2] // 2
    tm = min(tm, m)
    nb = n // tn
    return pl.pallas_call(
        _glu_kernel,
        grid=(nb, m // tm),
        in_specs=[pl.BlockSpec((tm, k), lambda j, i: (i, 0)),
                  pl.BlockSpec((None, k, tn), lambda j, i: (layer, 0, j)),
                  pl.BlockSpec((None, k, tn), lambda j, i: (layer, 0, nb + j))],
        out_specs=pl.BlockSpec((tm, tn), lambda j, i: (i, j)),
        out_shape=jax.ShapeDtypeStruct((m, n), jnp.bfloat16),
        compiler_params=_cparams("parallel", "parallel"),
        name="glu",
    )(y, w, w)


def _up_merge_kernel(oa_ref, ob_ref, wa_ref, wb_ref, ga_ref, gb_ref, o_ref):
    m = (_sigmoid(ga_ref[...]) * _dot(oa_ref[...], wa_ref[...])
         + _sigmoid(gb_ref[...]) * _dot(ob_ref[...], wb_ref[...]))
    o_ref[...] = m.astype(o_ref.dtype)


def _up_merge(oa, ob, wa, wb, layer, proj, tm, tn):
    m, k = oa.shape
    n = wa.shape[2]
    tm = min(tm, m)
    ga0 = (4 * A_WIDTH + B_WIDTH) // tn
    gb0 = ga0 + D_MODEL // tn
    return pl.pallas_call(
        _up_merge_kernel,
        grid=(n // tn, m // tm),
        in_specs=[pl.BlockSpec((tm, k), lambda j, i: (i, 0)),
                  pl.BlockSpec((tm, k), lambda j, i: (i, 0)),
                  pl.BlockSpec((None, k, tn), lambda j, i: (layer, 0, j)),
                  pl.BlockSpec((None, k, tn), lambda j, i: (layer, 0, j)),
                  pl.BlockSpec((tm, tn), lambda j, i: (i, ga0 + j)),
                  pl.BlockSpec((tm, tn), lambda j, i: (i, gb0 + j))],
        out_specs=pl.BlockSpec((tm, tn), lambda j, i: (i, j)),
        out_shape=jax.ShapeDtypeStruct((m, n), jnp.bfloat16),
        compiler_params=_cparams("parallel", "parallel"),
        name="up_merge",
    )(oa, ob, wa, wb, proj, proj)


def _wo_ln_kernel(m_ref, w_ref, x_ref, g_ref, b_ref, o_ref, ob_ref):
    z = ALPHA * x_ref[...] + _dot(m_ref[...], w_ref[...])
    y = _layer_norm(z, g_ref[...], b_ref[...])
    o_ref[...] = y
    ob_ref[...] = _bf(y)


def _wo_ln(merged, w, layer, x, g, b, tm):
    m, k = merged.shape
    n = w.shape[2]
    tm = min(tm, m)
    row = lambda i: (i, 0)
    fix = lambda i: (0, 0)
    return pl.pallas_call(
        _wo_ln_kernel,
        grid=(m // tm,),
        in_specs=[pl.BlockSpec((tm, k), row), pl.BlockSpec((None, k, n), lambda i: (layer, 0, 0)),
                  pl.BlockSpec((tm, n), row), pl.BlockSpec((1, n), fix),
                  pl.BlockSpec((1, n), fix)],
        out_specs=[pl.BlockSpec((tm, n), row), pl.BlockSpec((tm, n), row)],
        out_shape=[jax.ShapeDtypeStruct((m, n), jnp.float32),
                   jax.ShapeDtypeStruct((m, n), jnp.bfloat16)],
        compiler_params=_cparams("parallel"),
        name="wo_ln",
    )(merged, w, x, g, b)


def _add_ln_kernel(x_ref, y_ref, g_ref, b_ref, o_ref, ob_ref):
    y = _layer_norm(ALPHA * x_ref[...] + y_ref[...], g_ref[...], b_ref[...])
    o_ref[...] = y
    ob_ref[...] = _bf(y)


def _add_ln(x, y, g, b, tm):
    m, n = x.shape
    tm = min(tm, m)
    row = lambda i: (i, 0)
    fix = lambda i: (0, 0)
    return pl.pallas_call(
        _add_ln_kernel,
        grid=(m // tm,),
        in_specs=[pl.BlockSpec((tm, n), row), pl.BlockSpec((tm, n), row),
                  pl.BlockSpec((1, n), fix), pl.BlockSpec((1, n), fix)],
        out_specs=[pl.BlockSpec((tm, n), row), pl.BlockSpec((tm, n), row)],
        out_shape=[jax.ShapeDtypeStruct((m, n), jnp.float32),
                   jax.ShapeDtypeStruct((m, n), jnp.bfloat16)],
        compiler_params=_cparams("parallel"),
        name="add_ln",
    )(x, y, g, b)


def _take_top(s, n_take, on_take):
    rows = s.shape[0]
    rid = lax.broadcasted_iota(jnp.int32, s.shape, 0)
    for k in range(n_take):
        m = jnp.max(s, axis=0, keepdims=True)
        idx = jnp.min(jnp.where(s == m, rid, rows), axis=0, keepdims=True)
        on_take(k, m, idx)
        s = jnp.where(rid == idx, -jnp.inf, s)


def _peer_route_kernel(q_ref, keys_ref, r2_ref, e2_ref, n_ref, c_ref):
    tt = q_ref.shape[0]
    kid = lax.broadcasted_iota(jnp.int32, (P_NKEYS, tt), 0)
    rank_id = lax.broadcasted_iota(jnp.int32, (P_TOPK, tt), 0)
    for h in range(P_HEADS):
        s = []
        for half in range(2):
            qh = q_ref[:, (2 * h + half) * P_HALF:(2 * h + half + 1) * P_HALF]
            s.append(lax.dot_general(_bf(keys_ref[h, half]), _bf(qh), _NT,
                                     preferred_element_type=jnp.float32))
        vals = [[], []]
        idxs = [[], []]
        rank2 = [jnp.full((P_NKEYS, tt), P_TOPK, jnp.int32)]
        for half in range(2):
            def take(k, m, idx, half=half):
                vals[half].append(m)
                idxs[half].append(idx)
                if half == 1:
                    rank2[0] = jnp.where(kid == idx, k, rank2[0])
            _take_top(s[half], P_TOPK, take)
        v1 = jnp.concatenate(vals[0], axis=0)
        v2 = jnp.concatenate(vals[1], axis=0)
        widths = [P_TOPK // (i + 1) for i in range(P_TOPK)]
        starts = np.cumsum([0] + widths)
        pad = int(-starts[-1] % 8)
        cand = jnp.concatenate([v1[i:i + 1] + v2[:widths[i]] for i in range(P_TOPK)]
                               + [jnp.full((pad, tt), -jnp.inf, jnp.float32)], axis=0)
        state = [jnp.zeros((P_TOPK, tt), jnp.int32), jnp.zeros((1, tt), jnp.float32)]
        top = v1[0:1] + v2[0:1]

        def take_c(k, m, pos):
            rank1 = sum((pos >= int(st)).astype(jnp.int32) for st in starts[1:P_TOPK])
            state[0] = state[0] + (rank_id == rank1).astype(jnp.int32)
            state[1] = state[1] + jnp.exp(m - top)
        _take_top(cand, P_TOPK, take_c)
        n_rank, z = state
        n_key = jnp.zeros((P_NKEYS, tt), jnp.int32)
        for i in range(P_TOPK):
            n_key = jnp.where(kid == idxs[0][i], n_rank[i:i + 1], n_key)
        r2_ref[h] = rank2[0].astype(jnp.float32)
        n_ref[h] = n_key.astype(jnp.float32)
        e2_ref[h] = jnp.exp(s[1] - v2[0:1])
        c_ref[h] = jnp.exp(s[0] - v1[0:1]) / z


def _peer_route(q, keys, tt):
    n_tok = q.shape[0]
    tt = min(tt, n_tok)
    out = jax.ShapeDtypeStruct((P_HEADS, P_NKEYS, n_tok), jnp.float32)
    ospec = pl.BlockSpec((P_HEADS, P_NKEYS, tt), lambda i: (0, 0, i))
    return pl.pallas_call(
        _peer_route_kernel,
        grid=(n_tok // tt,),
        in_specs=[pl.BlockSpec((tt, P_HEADS * P_QDIM), lambda i: (i, 0)),
                  pl.BlockSpec((P_HEADS, 2, P_NKEYS, P_HALF), lambda i: (0, 0, 0, 0))],
        out_specs=[ospec] * 4,
        out_shape=[out] * 4,
        compiler_params=_cparams("parallel"),
        name="peer_route",
    )(q, keys)


def _peer_ffn_kernel(x_ref, u_ref, vt_ref, r2_ref, e2_ref, n_ref, c_ref, y_ref, acc_ref, h_ref):
    j = pl.program_id(1)
    eb = u_ref.shape[0]
    tm = x_ref.shape[0]

    @pl.when(j == 0)
    def _():
        acc_ref[...] = jnp.zeros_like(acc_ref)

    act = lax.dot_general(u_ref[...], x_ref[...], _NT, preferred_element_type=jnp.float32)
    for al in range(eb // P_NKEYS):
        a = j * (eb // P_NKEYS) + al
        rows = slice(al * P_NKEYS, (al + 1) * P_NKEYS)
        for cc in range(tm // GATE_COLS):
            cols = slice(cc * GATE_COLS, (cc + 1) * GATE_COLS)
            gate = None
            for h in range(P_HEADS):
                n_row = n_ref[h, pl.ds(a, 1), :][:, cols]
                c_row = c_ref[h, pl.ds(a, 1), :][:, cols]
                term = jnp.where(r2_ref[h, :, cols] < n_row, e2_ref[h, :, cols] * c_row, 0.0)
                gate = term if gate is None else gate + term
            h_ref[rows, cols] = (_gelu(act[rows, cols]) * gate).astype(h_ref.dtype)
    acc_ref[...] += _dot(vt_ref[...], h_ref[...])

    @pl.when(j == pl.num_programs(1) - 1)
    def _():
        y_ref[...] = acc_ref[...].T


def _peer_ffn(xb, u, vt, layer, route, tm, eb):
    n_tok, d = xb.shape
    tm = min(tm, n_tok)
    rspec = pl.BlockSpec((P_HEADS, P_NKEYS, tm), lambda i, j: (0, 0, i))
    return pl.pallas_call(
        _peer_ffn_kernel,
        grid=(n_tok // tm, P_NEXP // eb),
        in_specs=[pl.BlockSpec((tm, d), lambda i, j: (i, 0)),
                  pl.BlockSpec((None, eb, d), lambda i, j: (layer, j, 0)),
                  pl.BlockSpec((None, d, eb), lambda i, j: (layer, 0, j)),
                  rspec, rspec, rspec, rspec],
        out_specs=pl.BlockSpec((tm, d), lambda i, j: (i, 0)),
        out_shape=jax.ShapeDtypeStruct((n_tok, d), jnp.float32),
        scratch_shapes=[pltpu.VMEM((d, tm), jnp.float32),
                        pltpu.VMEM((eb, tm), u.dtype)],
        compiler_params=_cparams("parallel", "arbitrary"),
        name="peer_ffn",
    )(xb, u, vt, *route)


def _chunk_tri():
    i = np.arange(LANES)
    same = (i[:, None] // HG_CHUNK) == (i[None, :] // HG_CHUNK)
    return jnp.asarray(same & (i[None, :] <= i[:, None]), jnp.bfloat16)


def kernel(x, w_in, hgrn_lb_logits, hgrn_norm_g, s5_lambda_re, s5_lambda_im, s5_log_step,
           s5_b_re, s5_b_im, s5_c_re, s5_c_im, s5_d, s5_w_glu, w_up_a, w_up_b, w_o,
           ln1_g, ln1_b, peer_w_q, peer_keys, peer_u, peer_v, ln2_g, ln2_b):
    bsz, n_tok, d = x.shape
    assert bsz == 1 and d == D_MODEL and n_tok % (S5_CHUNK * 8) == 0
    f32 = jnp.float32
    p = jax.nn.softmax(hgrn_lb_logits.astype(f32), axis=0)
    c = jnp.cumsum(p, axis=0)
    lbs = c - c[0:1]
    loglb = jnp.log(lbs)
    log1mlb = jnp.log1p(-lbs)
    tri = _chunk_tri()

    w_in_b, w_glu_b, w_ua_b, w_ub_b, w_o_b, w_q_b, u_b = (
        _bf(w) for w in (w_in, s5_w_glu, w_up_a, w_up_b, w_o, peer_w_q, peer_u))
    vt_b = _bf(jnp.swapaxes(peer_v, 1, 2))
    row = lambda p, l: p[l][None].astype(f32)

    xf = x.reshape(n_tok, d).astype(f32)
    xb = _bf(xf)
    for l in range(DEPTH):
        proj = _matmul(xb, w_in_b, l, f32, tm=512, tn=1024)
        oa = _hgrn2(proj, loglb[l:l + 1], log1mlb[l:l + 1], row(hgrn_norm_g, l), tri, tb=512)
        tables = _s5_tables(s5_lambda_re[l], s5_lambda_im[l], s5_log_step[l], s5_b_re[l],
                            s5_b_im[l], s5_c_re[l], s5_c_im[l], s5_d[l])
        yb = _s5(proj, tables, cb=64, nrb=2)
        ob = _glu(yb, w_glu_b, l, tm=1024, tn=512)
        merged = _up_merge(oa, ob, w_ua_b, w_ub_b, l, proj, tm=1024, tn=512)
        xf, xb = _wo_ln(merged, w_o_b, l, xf, row(ln1_g, l), row(ln1_b, l), tm=256)
        q = _matmul(xb, w_q_b, l, f32, tm=512, tn=1024)
        route = _peer_route(q, peer_keys[l].astype(f32), tt=128)
        y = _peer_ffn(xb, u_b, vt_b, l, route, tm=512, eb=512)
        xf, xb = _add_ln(xf, y, row(ln2_g, l), row(ln2_b, l), tm=256)
    return xf.reshape(bsz, n_tok, d).astype(x.dtype)
```

```python
import math

import jax
import jax.numpy as jnp
import numpy as np
from jax import lax
from jax.experimental import pallas as pl
from jax.experimental.pallas import tpu as pltpu

D_MODEL = 2048
DEPTH = 4
A_HEADS = 8
A_DIM = 128
A_WIDTH = A_HEADS * A_DIM
HG_CHUNK = 16
HG_UNROLL = 8
B_GROUPS = 64
B_GROUP_CH = 16
B_STATE = 64
B_WIDTH = B_GROUPS * B_GROUP_CH
S5_CHUNK = 16
S5_SLAB_GROUPS = 8
S5_EIG_CLIP = -1e-4
IN_COLS = 4 * A_WIDTH + B_WIDTH + 2 * D_MODEL
P_HEADS = 8
P_QDIM = 256
P_HALF = 128
P_NKEYS = 128
P_NEXP = P_NKEYS * P_NKEYS
P_TOPK = 16
ALPHA = (2.0 * DEPTH) ** 0.25
LN_EPS = 1e-5
RMS_EPS = 1e-6

LANES = 128
GATE_COLS = 128
VMEM_LIMIT = 52 * 1024 * 1024

_NT = (((1,), (1,)), ((), ()))
_TN = (((0,), (0,)), ((), ()))


def _cparams(*sem):
    return pltpu.CompilerParams(dimension_semantics=sem, vmem_limit_bytes=VMEM_LIMIT)


def _bf(x):
    return x.astype(jnp.bfloat16)


def _dot(a, b):
    return jnp.dot(a, b, preferred_element_type=jnp.float32)


def _sigmoid(x):
    return 1.0 / (1.0 + jnp.exp(-x))


def _gelu(x):
    c = math.sqrt(2.0 / math.pi)
    return 0.5 * x * (1.0 + jnp.tanh(c * (x + 0.044715 * (x * x * x))))


def _layer_norm(z, g, b):
    mu = jnp.mean(z, axis=-1, keepdims=True)
    zc = z - mu
    var = jnp.mean(zc * zc, axis=-1, keepdims=True)
    return zc * lax.rsqrt(var + LN_EPS) * g + b


def _matmul_kernel(a_ref, w_ref, o_ref):
    o_ref[...] = _dot(a_ref[...], w_ref[...]).astype(o_ref.dtype)


def _matmul(a, w, layer, out_dtype, tm, tn):
    m, k = a.shape
    n = w.shape[2]
    tm = min(tm, m)
    return pl.pallas_call(
        _matmul_kernel,
        grid=(n // tn, m // tm),
        in_specs=[pl.BlockSpec((tm, k), lambda j, i: (i, 0)),
                  pl.BlockSpec((None, k, tn), lambda j, i: (layer, 0, j))],
        out_specs=pl.BlockSpec((tm, tn), lambda j, i: (i, j)),
        out_shape=jax.ShapeDtypeStruct((m, n), out_dtype),
        compiler_params=_cparams("parallel", "parallel"),
        name="matmul",
    )(a, w)


def _hgrn2_kernel(q_ref, f_ref, i_ref, g_ref, loglb_ref, log1mlb_ref, ng_ref, tri_ref,
                  o_ref, st_ref, b_ref, k_ref):
    tb = q_ref.shape[0]

    @pl.when(pl.program_id(1) == 0)
    def _():
        st_ref[...] = jnp.zeros_like(st_ref)

    fz = f_ref[...]
    log_sig = jnp.minimum(fz, 0.0) - jnp.log1p(jnp.exp(-jnp.abs(fz)))
    z = log1mlb_ref[...] + log_sig
    a = loglb_ref[...]
    logf = jnp.maximum(a, z) + jnp.log1p(jnp.exp(-jnp.abs(a - z)))
    k_ref[...] = 1.0 - jnp.exp(logf)
    tri = tri_ref[...]
    for r in range(tb // LANES):
        rows = slice(r * LANES, (r + 1) * LANES)
        lf = logf[rows]
        hi = _bf(lf)
        r1 = lf - hi.astype(jnp.float32)
        mid = _bf(r1)
        lo = _bf(r1 - mid.astype(jnp.float32))
        b_ref[rows, :] = _dot(tri, hi) + _dot(tri, mid) + _dot(tri, lo)

    srow = lax.broadcasted_iota(jnp.int32, (HG_CHUNK, 1), 0)
    ng = ng_ref[...]

    def chunk(j, st):
        rows = pl.ds(pl.multiple_of(j * HG_CHUNK, HG_CHUNK), HG_CHUNK)
        b = b_ref[rows, :]
        q = q_ref[rows, :]
        k = k_ref[rows, :]
        v = i_ref[rows, :]
        b_last = b[HG_CHUNK - 1:HG_CHUNK, :]
        o = lax.dot_general(_bf(q * jnp.exp(b)), _bf(st), _NT,
                            preferred_element_type=jnp.float32)
        intra = []
        for t in range(HG_CHUNK):
            diff = jnp.where(srow <= t, b[t:t + 1, :] - b, -jnp.inf)
            p = (q[t:t + 1, :] * k) * jnp.exp(diff)
            att = jnp.sum(p, axis=-1, keepdims=True)
            intra.append(jnp.sum(att * v, axis=0, keepdims=True))
        o = o + jnp.concatenate(intra, axis=0)
        k_tail = k * jnp.exp(b_last - b)
        st = st * jnp.exp(b_last) + lax.dot_general(
            _bf(v), _bf(k_tail), _TN, preferred_element_type=jnp.float32)
        o = o * lax.rsqrt(jnp.mean(o * o, axis=-1, keepdims=True) + RMS_EPS)
        o = o * ng * _sigmoid(g_ref[rows, :])
        o_ref[rows, :] = o.astype(o_ref.dtype)
        return st

    st_ref[...] = lax.fori_loop(0, tb // HG_CHUNK, chunk, st_ref[...], unroll=HG_UNROLL)


def _hgrn2(proj, loglb, log1mlb, norm_g, tri, tb):
    n_tok = proj.shape[0]
    tb = min(tb, n_tok)

    def col(off):
        return pl.BlockSpec((tb, A_DIM), lambda h, i: (i, off + h))

    def par():
        return pl.BlockSpec((1, A_DIM), lambda h, i: (0, h))

    return pl.pallas_call(
        _hgrn2_kernel,
        grid=(A_HEADS, n_tok // tb),
        in_specs=[col(0), col(A_HEADS), col(2 * A_HEADS), col(3 * A_HEADS),
                  par(), par(), par(),
                  pl.BlockSpec((LANES, LANES), lambda h, i: (0, 0))],
        out_specs=pl.BlockSpec((tb, A_DIM), lambda h, i: (i, h)),
        out_shape=jax.ShapeDtypeStruct((n_tok, A_WIDTH), jnp.bfloat16),
        scratch_shapes=[pltpu.VMEM((A_DIM, A_DIM), jnp.float32),
                        pltpu.VMEM((tb, A_DIM), jnp.float32),
                        pltpu.VMEM((tb, A_DIM), jnp.float32)],
        compiler_params=_cparams("parallel", "arbitrary"),
        name="hgrn2",
    )(proj, proj, proj, proj, loglb, log1mlb, norm_g, tri)


def _s5_tables(lam_re, lam_im, log_step, b_re, b_im, c_re, c_im, d):
    f32 = jnp.float32
    hp = lax.Precision.HIGHEST
    t = S5_CHUNK
    lr = jnp.minimum(lam_re.astype(f32), S5_EIG_CLIP)
    li = lam_im.astype(f32)
    dt = jnp.exp(log_step.astype(f32))[:, None]
    mag = jnp.exp(lr * dt)
    ar = mag * jnp.cos(li * dt)
    ai = mag * jnp.sin(li * dt)
    den = lr * lr + li * li
    nr = ar - 1.0
    zr = (nr * lr + ai * li) / den
    zi = (ai * lr - nr * li) / den
    br_, bi_ = b_re.astype(f32), b_im.astype(f32)
    bbr = zr[..., None] * br_ - zi[..., None] * bi_
    bbi = zr[..., None] * bi_ + zi[..., None] * br_
    pr = [jnp.ones_like(ar)]
    pi = [jnp.zeros_like(ai)]
    for _ in range(t):
        r0, i0 = pr[-1], pi[-1]
        pr.append(r0 * ar - i0 * ai)
        pi.append(r0 * ai + i0 * ar)
    pwr = jnp.stack(pr, axis=1)
    pwi = jnp.stack(pi, axis=1)
    cr, ci = c_re.astype(f32), c_im.astype(f32)
    car = cr[:, None] * pwr[:, :t, None, :] - ci[:, None] * pwi[:, :t, None, :]
    cai = cr[:, None] * pwi[:, :t, None, :] + ci[:, None] * pwr[:, :t, None, :]
    kk = (jnp.einsum('gtnp,gpm->gtnm', car, bbr, precision=hp)
          - jnp.einsum('gtnp,gpm->gtnm', cai, bbi, precision=hp))
    ti = np.arange(t)
    lag = ti[None, :] - ti[:, None]
    toep = kk[:, np.clip(lag, 0, t - 1)]
    toep = jnp.where((lag >= 0)[None, :, :, None, None], toep, 0.0)
    sg = S5_SLAB_GROUPS
    ns = B_GROUPS // sg
    same = np.eye(sg, dtype=bool)

    def spread(compact, g_axis):
        shape = [1] * (compact.ndim + 1)
        shape[g_axis] = sg
        shape[-2] = sg
        return jnp.where(same.reshape(shape), compact[..., None, :], 0.0)

    toep = jnp.transpose(toep.reshape(ns, sg, t, t, B_GROUP_CH, B_GROUP_CH), (0, 2, 1, 5, 3, 4))
    toep = _bf(spread(toep, 2)).reshape(ns, t * LANES, t * LANES)
    er = pwr[:, t - 1 - ti][:, :, None, :]
    ei = pwi[:, t - 1 - ti][:, :, None, :]
    bbr_t = jnp.transpose(bbr, (0, 2, 1))[:, None]
    bbi_t = jnp.transpose(bbi, (0, 2, 1))[:, None]
    w1 = jnp.concatenate([er * bbr_t - ei * bbi_t, er * bbi_t + ei * bbr_t], axis=-1)
    w1 = jnp.transpose(w1.reshape(ns, sg, t, B_GROUP_CH, 2 * B_STATE), (0, 2, 1, 3, 4))
    w1 = _bf(spread(w1, 2)).reshape(ns, t * LANES, sg * 2 * B_STATE)
    zr1 = pwr[:, 1:, None, :]
    zi1 = pwi[:, 1:, None, :]
    w2r = cr[:, None] * zr1 - ci[:, None] * zi1
    w2i = -(cr[:, None] * zi1 + ci[:, None] * zr1)
    w2 = jnp.concatenate([w2r, w2i], axis=-1)
    w2 = jnp.transpose(w2.reshape(ns, sg, t, B_GROUP_CH, 2 * B_STATE), (0, 1, 4, 2, 3))
    w2 = _bf(spread(w2, 1)).reshape(ns, sg * 2 * B_STATE, t * LANES)
    a1 = jnp.concatenate([pwr[:, t], pwr[:, t]], axis=-1)
    a2 = jnp.concatenate([-pwi[:, t], pwi[:, t]], axis=-1)
    dd = jnp.tile(d.astype(f32).reshape(ns, 1, LANES), (1, 1, t))
    return toep, w1, w2, a1, a2, dd


def _s5_rows(u_ref, nc):
    return jnp.concatenate([u_ref[pl.ds(t, nc, stride=S5_CHUNK), :] for t in range(S5_CHUNK)],
                           axis=1)


def _s5_local_kernel(u_ref, w1_ref, e_ref):
    e_ref[...] = _dot(_bf(_s5_rows(u_ref, e_ref.shape[0])), w1_ref[0])


def _s5_scan_kernel(e_ref, a1_ref, a2_ref, xp_ref, st_ref):
    @pl.when(pl.program_id(0) == 0)
    def _():
        st_ref[...] = jnp.zeros_like(st_ref)

    a1 = a1_ref[...]
    a2 = a2_ref[...]

    def step(c, carry):
        x = st_ref[...]
        xp_ref[c] = x
        st_ref[...] = a1 * x + a2 * pltpu.roll(x, B_STATE, 1) + e_ref[c]
        return carry

    lax.fori_loop(0, e_ref.shape[0], step, 0)


def _s5_out_kernel(u_ref, xp_ref, toep_ref, w2_ref, d_ref, y_ref):
    nc = xp_ref.shape[0]
    x = _s5_rows(u_ref, nc)
    y = _dot(_bf(x), toep_ref[0]) + _dot(_bf(xp_ref[...]), w2_ref[0]) + d_ref[0] * x
    y = _gelu(y)
    for t in range(S5_CHUNK):
        y_ref[pl.ds(t, nc, stride=S5_CHUNK), :] = y[:, t * LANES:(t + 1) * LANES]


def _s5(proj, tables, cb, nrb):
    toep, w1, w2, a1, a2, dd = tables
    n_tok = proj.shape[0]
    nc = n_tok // S5_CHUNK
    g = B_GROUPS
    ns = 2 * B_STATE
    n_slab = B_WIDTH // LANES
    sw = S5_SLAB_GROUPS * ns
    u0 = 4 * A_WIDTH // LANES
    cb = min(cb, nc)
    ncb = nc // nrb
    tokb = n_tok // nrb
    e = pl.pallas_call(
        _s5_local_kernel,
        grid=(n_slab, nrb),
        in_specs=[pl.BlockSpec((tokb, LANES), lambda s, r: (r, u0 + s)),
                  pl.BlockSpec((1, S5_CHUNK * LANES, sw), lambda s, r: (s, 0, 0))],
        out_specs=pl.BlockSpec((ncb, sw), lambda s, r: (r, s)),
        out_shape=jax.ShapeDtypeStruct((nc, g * ns), jnp.float32),
        compiler_params=_cparams("parallel", "parallel"),
        name="s5_local",
    )(proj, w1)
    xp = pl.pallas_call(
        _s5_scan_kernel,
        grid=(nc // cb,),
        in_specs=[pl.BlockSpec((cb, g, ns), lambda i: (i, 0, 0)),
                  pl.BlockSpec((g, ns), lambda i: (0, 0)),
                  pl.BlockSpec((g, ns), lambda i: (0, 0))],
        out_specs=pl.BlockSpec((cb, g, ns), lambda i: (i, 0, 0)),
        out_shape=jax.ShapeDtypeStruct((nc, g, ns), jnp.float32),
        scratch_shapes=[pltpu.VMEM((g, ns), jnp.float32)],
        compiler_params=_cparams("arbitrary"),
        name="s5_scan",
    )(e.reshape(nc, g, ns), a1, a2)
    return pl.pallas_call(
        _s5_out_kernel,
        grid=(n_slab, nrb),
        in_specs=[pl.BlockSpec((tokb, LANES), lambda s, r: (r, u0 + s)),
                  pl.BlockSpec((ncb, sw), lambda s, r: (r, s)),
                  pl.BlockSpec((1, S5_CHUNK * LANES, S5_CHUNK * LANES), lambda s, r: (s, 0, 0)),
                  pl.BlockSpec((1, sw, S5_CHUNK * LANES), lambda s, r: (s, 0, 0)),
                  pl.BlockSpec((1, 1, S5_CHUNK * LANES), lambda s, r: (s, 0, 0))],
        out_specs=pl.BlockSpec((tokb, LANES), lambda s, r: (r, s)),
        out_shape=jax.ShapeDtypeStruct((n_tok, B_WIDTH), jnp.float32),
        compiler_params=_cparams("parallel", "parallel"),
        name="s5_out",
    )(proj, xp.reshape(nc, g * ns), toep, w2, dd)


def _glu_kernel(y_ref, wa_ref, wb_ref, o_ref):
    y = _bf(y_ref[...])
    o_ref[...] = (_dot(y, wa_ref[...]) * _sigmoid(_dot(y, wb_ref[...]))).astype(o_ref.dtype)


def _glu(y, w, layer, tm, tn):
    m, k = y.shape
    n = w.shape[2] // 2
    tm = min(tm, m)
    nb = n // tn
    return pl.pallas_call(
        _glu_kernel,
        grid=(nb, m // tm),
        in_specs=[pl.BlockSpec((tm, k), lambda j, i: (i, 0)),
                  pl.BlockSpec((None, k, tn), lambda j, i: (layer, 0, j)),
                  pl.BlockSpec((None, k, tn), lambda j, i: (layer, 0, nb + j))],
        out_specs=pl.BlockSpec((tm, tn), lambda j, i: (i, j)),
        out_shape=jax.ShapeDtypeStruct((m, n), jnp.bfloat16),
        compiler_params=_cparams("parallel", "parallel"),
        name="glu",
    )(y, w, w)


def _up_merge_kernel(oa_ref, ob_ref, wa_ref, wb_ref, ga_ref, gb_ref, o_ref):
    m = (_sigmoid(ga_ref[...]) * _dot(oa_ref[...], wa_ref[...])
         + _sigmoid(gb_ref[...]) * _dot(ob_ref[...], wb_ref[...]))
    o_ref[...] = m.astype(o_ref.dtype)


def _up_merge(oa, ob, wa, wb, layer, proj, tm, tn):
    m, k = oa.shape
    n = wa.shape[2]
    tm = min(tm, m)
    ga0 = (4 * A_WIDTH + B_WIDTH) // tn
    gb0 = ga0 + D_MODEL // tn
    return pl.pallas_call(
        _up_merge_kernel,
        grid=(n // tn, m // tm),
        in_specs=[pl.BlockSpec((tm, k), lambda j, i: (i, 0)),
                  pl.BlockSpec((tm, k), lambda j, i: (i, 0)),
                  pl.BlockSpec((None, k, tn), lambda j, i: (layer, 0, j)),
                  pl.BlockSpec((None, k, tn), lambda j, i: (layer, 0, j)),
                  pl.BlockSpec((tm, tn), lambda j, i: (i, ga0 + j)),
                  pl.BlockSpec((tm, tn), lambda j, i: (i, gb0 + j))],
        out_specs=pl.BlockSpec((tm, tn), lambda j, i: (i, j)),
        out_shape=jax.ShapeDtypeStruct((m, n), jnp.bfloat16),
        compiler_params=_cparams("parallel", "parallel"),
        name="up_merge",
    )(oa, ob, wa, wb, proj, proj)


def _wo_ln_kernel(m_ref, w_ref, x_ref, g_ref, b_ref, o_ref, ob_ref):
    z = ALPHA * x_ref[...] + _dot(m_ref[...], w_ref[...])
    y = _layer_norm(z, g_ref[...], b_ref[...])
    o_ref[...] = y
    ob_ref[...] = _bf(y)


def _wo_ln(merged, w, layer, x, g, b, tm):
    m, k = merged.shape
    n = w.shape[2]
    tm = min(tm, m)
    row = lambda i: (i, 0)
    fix = lambda i: (0, 0)
    return pl.pallas_call(
        _wo_ln_kernel,
        grid=(m // tm,),
        in_specs=[pl.BlockSpec((tm, k), row), pl.BlockSpec((None, k, n), lambda i: (layer, 0, 0)),
                  pl.BlockSpec((tm, n), row), pl.BlockSpec((1, n), fix),
                  pl.BlockSpec((1, n), fix)],
        out_specs=[pl.BlockSpec((tm, n), row), pl.BlockSpec((tm, n), row)],
        out_shape=[jax.ShapeDtypeStruct((m, n), jnp.float32),
                   jax.ShapeDtypeStruct((m, n), jnp.bfloat16)],
        compiler_params=_cparams("parallel"),
        name="wo_ln",
    )(merged, w, x, g, b)


def _add_ln_kernel(x_ref, y_ref, g_ref, b_ref, o_ref, ob_ref):
    y = _layer_norm(ALPHA * x_ref[...] + y_ref[...], g_ref[...], b_ref[...])
    o_ref[...] = y
    ob_ref[...] = _bf(y)


def _add_ln(x, y, g, b, tm):
    m, n = x.shape
    tm = min(tm, m)
    row = lambda i: (i, 0)
    fix = lambda i: (0, 0)
    return pl.pallas_call(
        _add_ln_kernel,
        grid=(m // tm,),
        in_specs=[pl.BlockSpec((tm, n), row), pl.BlockSpec((tm, n), row),
                  pl.BlockSpec((1, n), fix), pl.BlockSpec((1, n), fix)],
        out_specs=[pl.BlockSpec((tm, n), row), pl.BlockSpec((tm, n), row)],
        out_shape=[jax.ShapeDtypeStruct((m, n), jnp.float32),
                   jax.ShapeDtypeStruct((m, n), jnp.bfloat16)],
        compiler_params=_cparams("parallel"),
        name="add_ln",
    )(x, y, g, b)


def _take_top(s, n_take, on_take):
    rows = s.shape[0]
    rid = lax.broadcasted_iota(jnp.int32, s.shape, 0)
    for k in range(n_take):
        m = jnp.max(s, axis=0, keepdims=True)
        idx = jnp.min(jnp.where(s == m, rid, rows), axis=0, keepdims=True)
        on_take(k, m, idx)
        s = jnp.where(rid == idx, -jnp.inf, s)


def _peer_route_kernel(q_ref, keys_ref, r2_ref, e2_ref, n_ref, c_ref):
    tt = q_ref.shape[0]
    kid = lax.broadcasted_iota(jnp.int32, (P_NKEYS, tt), 0)
    rank_id = lax.broadcasted_iota(jnp.int32, (P_TOPK, tt), 0)
    for h in range(P_HEADS):
        s = []
        for half in range(2):
            qh = q_ref[:, (2 * h + half) * P_HALF:(2 * h + half + 1) * P_HALF]
            s.append(lax.dot_general(_bf(keys_ref[h, half]), _bf(qh), _NT,
                                     preferred_element_type=jnp.float32))
        vals = [[], []]
        idxs = [[], []]
        rank2 = [jnp.full((P_NKEYS, tt), P_TOPK, jnp.int32)]
        for half in range(2):
            def take(k, m, idx, half=half):
                vals[half].append(m)
                idxs[half].append(idx)
                if half == 1:
                    rank2[0] = jnp.where(kid == idx, k, rank2[0])
            _take_top(s[half], P_TOPK, take)
        v1 = jnp.concatenate(vals[0], axis=0)
        v2 = jnp.concatenate(vals[1], axis=0)
        widths = [P_TOPK // (i + 1) for i in range(P_TOPK)]
        starts = np.cumsum([0] + widths)
        pad = int(-starts[-1] % 8)
        cand = jnp.concatenate([v1[i:i + 1] + v2[:widths[i]] for i in range(P_TOPK)]
                               + [jnp.full((pad, tt), -jnp.inf, jnp.float32)], axis=0)
        state = [jnp.zeros((P_TOPK, tt), jnp.int32), jnp.zeros((1, tt), jnp.float32)]
        top = v1[0:1] + v2[0:1]

        def take_c(k, m, pos):
            rank1 = sum((pos >= int(st)).astype(jnp.int32) for st in starts[1:P_TOPK])
            state[0] = state[0] + (rank_id == rank1).astype(jnp.int32)
            state[1] = state[1] + jnp.exp(m - top)
        _take_top(cand, P_TOPK, take_c)
        n_rank, z = state
        n_key = jnp.zeros((P_NKEYS, tt), jnp.int32)
        for i in range(P_TOPK):
            n_key = jnp.where(kid == idxs[0][i], n_rank[i:i + 1], n_key)
        r2_ref[h] = rank2[0].astype(jnp.float32)
        n_ref[h] = n_key.astype(jnp.float32)
        e2_ref[h] = jnp.exp(s[1] - v2[0:1])
        c_ref[h] = jnp.exp(s[0] - v1[0:1]) / z


def _peer_route(q, keys, tt):
    n_tok = q.shape[0]
    tt = min(tt, n_tok)
    out = jax.ShapeDtypeStruct((P_HEADS, P_NKEYS, n_tok), jnp.float32)
    ospec = pl.BlockSpec((P_HEADS, P_NKEYS, tt), lambda i: (0, 0, i))
    return pl.pallas_call(
        _peer_route_kernel,
        grid=(n_tok // tt,),
        in_specs=[pl.BlockSpec((tt, P_HEADS * P_QDIM), lambda i: (i, 0)),
                  pl.BlockSpec((P_HEADS, 2, P_NKEYS, P_HALF), lambda i: (0, 0, 0, 0))],
        out_specs=[ospec] * 4,
        out_shape=[out] * 4,
        compiler_params=_cparams("parallel"),
        name="peer_route",
    )(q, keys)


def _peer_ffn_kernel(xt_ref, u_ref, vt_ref, r2_ref, e2_ref, n_ref, c_ref, y_ref, acc_ref, h_ref):
    j = pl.program_id(1)
    eb = u_ref.shape[0]
    tm = xt_ref.shape[1]

    @pl.when(j == 0)
    def _():
        acc_ref[...] = jnp.zeros_like(acc_ref)

    act = _dot(u_ref[...], xt_ref[...])
    for al in range(eb // P_NKEYS):
        a = j * (eb // P_NKEYS) + al
        rows = slice(al * P_NKEYS, (al + 1) * P_NKEYS)
        for cc in range(tm // GATE_COLS):
            cols = slice(cc * GATE_COLS, (cc + 1) * GATE_COLS)
            gate = None
            for h in range(P_HEADS):
                n_row = n_ref[h, pl.ds(a, 1), :][:, cols]
                c_row = c_ref[h, pl.ds(a, 1), :][:, cols]
                term = jnp.where(r2_ref[h, :, cols] < n_row, e2_ref[h, :, cols] * c_row, 0.0)
                gate = term if gate is None else gate + term
            h_ref[rows, cols] = (_gelu(act[rows, cols]) * gate).astype(h_ref.dtype)
    acc_ref[...] += _dot(vt_ref[...], h_ref[...])

    @pl.when(j == pl.num_programs(1) - 1)
    def _():
        y_ref[...] = acc_ref[...].T


def _peer_ffn(xt, u, vt, layer, route, tm, eb):
    d, n_tok = xt.shape
    tm = min(tm, n_tok)
    rspec = pl.BlockSpec((P_HEADS, P_NKEYS, tm), lambda i, j: (0, 0, i))
    return pl.pallas_call(
        _peer_ffn_kernel,
        grid=(n_tok // tm, P_NEXP // eb),
        in_specs=[pl.BlockSpec((d, tm), lambda i, j: (0, i)),
                  pl.BlockSpec((None, eb, d), lambda i, j: (layer, j, 0)),
                  pl.BlockSpec((None, d, eb), lambda i, j: (layer, 0, j)),
                  rspec, rspec, rspec, rspec],
        out_specs=pl.BlockSpec((tm, d), lambda i, j: (i, 0)),
        out_shape=jax.ShapeDtypeStruct((n_tok, d), jnp.float32),
        scratch_shapes=[pltpu.VMEM((d, tm), jnp.float32),
                        pltpu.VMEM((eb, tm), u.dtype)],
        compiler_params=_cparams("parallel", "arbitrary"),
        name="peer_ffn",
    )(xt, u, vt, *route)


def _chunk_tri():
    i = np.arange(LANES)
    same = (i[:, None] // HG_CHUNK) == (i[None, :] // HG_CHUNK)
    return jnp.asarray(same & (i[None, :] <= i[:, None]), jnp.bfloat16)


def kernel(x, w_in, hgrn_lb_logits, hgrn_norm_g, s5_lambda_re, s5_lambda_im, s5_log_step,
           s5_b_re, s5_b_im, s5_c_re, s5_c_im, s5_d, s5_w_glu, w_up_a, w_up_b, w_o,
           ln1_g, ln1_b, peer_w_q, peer_keys, peer_u, peer_v, ln2_g, ln2_b):
    bsz, n_tok, d = x.shape
    assert bsz == 1 and d == D_MODEL and n_tok % (S5_CHUNK * 8) == 0
    f32 = jnp.float32
    p = jax.nn.softmax(hgrn_lb_logits.astype(f32), axis=0)
    c = jnp.cumsum(p, axis=0)
    lbs = c - c[0:1]
    loglb = jnp.log(lbs)
    log1mlb = jnp.log1p(-lbs)
    tri = _chunk_tri()

    w_in_b, w_glu_b, w_ua_b, w_ub_b, w_o_b, w_q_b, u_b = (
        _bf(w) for w in (w_in, s5_w_glu, w_up_a, w_up_b, w_o, peer_w_q, peer_u))
    vt_b = _bf(jnp.swapaxes(peer_v, 1, 2))
    row = lambda p, l: p[l][None].astype(f32)

    xf = x.reshape(n_tok, d).astype(f32)
    xb = _bf(xf)
    for l in range(DEPTH):
        proj = _matmul(xb, w_in_b, l, f32, tm=512, tn=1024)
        oa = _hgrn2(proj, loglb[l:l + 1], log1mlb[l:l + 1], row(hgrn_norm_g, l), tri, tb=512)
        tables = _s5_tables(s5_lambda_re[l], s5_lambda_im[l], s5_log_step[l], s5_b_re[l],
                            s5_b_im[l], s5_c_re[l], s5_c_im[l], s5_d[l])
        yb = _s5(proj, tables, cb=64, nrb=2)
        ob = _glu(yb, w_glu_b, l, tm=1024, tn=512)
        merged = _up_merge(oa, ob, w_ua_b, w_ub_b, l, proj, tm=1024, tn=512)
        xf, xb = _wo_ln(merged, w_o_b, l, xf, row(ln1_g, l), row(ln1_b, l), tm=256)
        q = _matmul(xb, w_q_b, l, f32, tm=512, tn=1024)
        route = _peer_route(q, peer_keys[l].astype(f32), tt=128)
        y = _peer_ffn(xb.T, u_b, vt_b, l, route, tm=512, eb=512)
        xf, xb = _add_ln(xf, y, row(ln2_g, l), row(ln2_b, l), tm=256)
    return xf.reshape(bsz, n_tok, d).astype(x.dtype)
```

```python
import math

import jax
import jax.numpy as jnp
import numpy as np
from jax import lax
from jax.experimental import pallas as pl
from jax.experimental.pallas import tpu as pltpu

D_MODEL = 2048
DEPTH = 4
A_HEADS = 8
A_DIM = 128
A_WIDTH = A_HEADS * A_DIM
HG_CHUNK = 16
HG_UNROLL = 8
B_GROUPS = 64
B_GROUP_CH = 16
B_STATE = 64
B_WIDTH = B_GROUPS * B_GROUP_CH
S5_CHUNK = 16
S5_SLAB_GROUPS = 8
S5_EIG_CLIP = -1e-4
IN_COLS = 4 * A_WIDTH + B_WIDTH + 2 * D_MODEL
P_HEADS = 8
P_QDIM = 256
P_HALF = 128
P_NKEYS = 128
P_NEXP = P_NKEYS * P_NKEYS
P_TOPK = 16
ALPHA = (2.0 * DEPTH) ** 0.25
LN_EPS = 1e-5
RMS_EPS = 1e-6

LANES = 128
GATE_COLS = 128
VMEM_LIMIT = 52 * 1024 * 1024

_NT = (((1,), (1,)), ((), ()))
_TN = (((0,), (0,)), ((), ()))


def _cparams(*sem):
    return pltpu.CompilerParams(dimension_semantics=sem, vmem_limit_bytes=VMEM_LIMIT)


def _bf(x):
    return x.astype(jnp.bfloat16)


def _dot(a, b):
    return jnp.dot(a, b, preferred_element_type=jnp.float32)


def _sigmoid(x):
    return 1.0 / (1.0 + jnp.exp(-x))


def _gelu(x):
    c = math.sqrt(2.0 / math.pi)
    return 0.5 * x * (1.0 + jnp.tanh(c * (x + 0.044715 * (x * x * x))))


def _layer_norm(z, g, b):
    mu = jnp.mean(z, axis=-1, keepdims=True)
    zc = z - mu
    var = jnp.mean(zc * zc, axis=-1, keepdims=True)
    return zc * lax.rsqrt(var + LN_EPS) * g + b


def _matmul_kernel(a_ref, w_ref, o_ref):
    o_ref[...] = _dot(a_ref[...], w_ref[...]).astype(o_ref.dtype)


def _matmul(a, w, layer, out_dtype, tm, tn):
    m, k = a.shape
    n = w.shape[2]
    tm = min(tm, m)
    return pl.pallas_call(
        _matmul_kernel,
        grid=(n // tn, m // tm),
        in_specs=[pl.BlockSpec((tm, k), lambda j, i: (i, 0)),
                  pl.BlockSpec((None, k, tn), lambda j, i: (layer, 0, j))],
        out_specs=pl.BlockSpec((tm, tn), lambda j, i: (i, j)),
        out_shape=jax.ShapeDtypeStruct((m, n), out_dtype),
        compiler_params=_cparams("parallel", "parallel"),
        name="matmul",
    )(a, w)


def _hgrn2_kernel(q_ref, f_ref, i_ref, g_ref, loglb_ref, log1mlb_ref, ng_ref, tri_ref,
                  o_ref, st_ref, b_ref, k_ref):
    tb = q_ref.shape[0]

    @pl.when(pl.program_id(1) == 0)
    def _():
        st_ref[...] = jnp.zeros_like(st_ref)

    fz = f_ref[...]
    log_sig = jnp.minimum(fz, 0.0) - jnp.log1p(jnp.exp(-jnp.abs(fz)))
    z = log1mlb_ref[...] + log_sig
    a = loglb_ref[...]
    logf = jnp.maximum(a, z) + jnp.log1p(jnp.exp(-jnp.abs(a - z)))
    k_ref[...] = 1.0 - jnp.exp(logf)
    tri = tri_ref[...]
    for r in range(tb // LANES):
        rows = slice(r * LANES, (r + 1) * LANES)
        lf = logf[rows]
        hi = _bf(lf)
        r1 = lf - hi.astype(jnp.float32)
        mid = _bf(r1)
        lo = _bf(r1 - mid.astype(jnp.float32))
        b_ref[rows, :] = _dot(tri, hi) + _dot(tri, mid) + _dot(tri, lo)

    srow = lax.broadcasted_iota(jnp.int32, (HG_CHUNK, 1), 0)
    ng = ng_ref[...]

    def chunk(j, st):
        rows = pl.ds(pl.multiple_of(j * HG_CHUNK, HG_CHUNK), HG_CHUNK)
        b = b_ref[rows, :]
        q = q_ref[rows, :]
        k = k_ref[rows, :]
        v = i_ref[rows, :]
        b_last = b[HG_CHUNK - 1:HG_CHUNK, :]
        o = lax.dot_general(_bf(q * jnp.exp(b)), _bf(st), _NT,
                            preferred_element_type=jnp.float32)
        intra = []
        for t in range(HG_CHUNK):
            diff = jnp.where(srow <= t, b[t:t + 1, :] - b, -jnp.inf)
            p = (q[t:t + 1, :] * k) * jnp.exp(diff)
            att = jnp.sum(p, axis=-1, keepdims=True)
            intra.append(jnp.sum(att * v, axis=0, keepdims=True))
        o = o + jnp.concatenate(intra, axis=0)
        k_tail = k * jnp.exp(b_last - b)
        st = st * jnp.exp(b_last) + lax.dot_general(
            _bf(v), _bf(k_tail), _TN, preferred_element_type=jnp.float32)
        o = o * lax.rsqrt(jnp.mean(o * o, axis=-1, keepdims=True) + RMS_EPS)
        o = o * ng * _sigmoid(g_ref[rows, :])
        o_ref[rows, :] = o.astype(o_ref.dtype)
        return st

    st_ref[...] = lax.fori_loop(0, tb // HG_CHUNK, chunk, st_ref[...], unroll=HG_UNROLL)


def _hgrn2(proj, loglb, log1mlb, norm_g, tri, tb):
    n_tok = proj.shape[0]
    tb = min(tb, n_tok)

    def col(off):
        return pl.BlockSpec((tb, A_DIM), lambda h, i: (i, off + h))

    def par():
        return pl.BlockSpec((1, A_DIM), lambda h, i: (0, h))

    return pl.pallas_call(
        _hgrn2_kernel,
        grid=(A_HEADS, n_tok // tb),
        in_specs=[col(0), col(A_HEADS), col(2 * A_HEADS), col(3 * A_HEADS),
                  par(), par(), par(),
                  pl.BlockSpec((LANES, LANES), lambda h, i: (0, 0))],
        out_specs=pl.BlockSpec((tb, A_DIM), lambda h, i: (i, h)),
        out_shape=jax.ShapeDtypeStruct((n_tok, A_WIDTH), jnp.bfloat16),
        scratch_shapes=[pltpu.VMEM((A_DIM, A_DIM), jnp.float32),
                        pltpu.VMEM((tb, A_DIM), jnp.float32),
                        pltpu.VMEM((tb, A_DIM), jnp.float32)],
        compiler_params=_cparams("parallel", "arbitrary"),
        name="hgrn2",
    )(proj, proj, proj, proj, loglb, log1mlb, norm_g, tri)


def _s5_tables(lam_re, lam_im, log_step, b_re, b_im, c_re, c_im, d):
    f32 = jnp.float32
    hp = lax.Precision.HIGHEST
    t = S5_CHUNK
    lr = jnp.minimum(lam_re.astype(f32), S5_EIG_CLIP)
    li = lam_im.astype(f32)
    dt = jnp.exp(log_step.astype(f32))[:, None]
    mag = jnp.exp(lr * dt)
    ar = mag * jnp.cos(li * dt)
    ai = mag * jnp.sin(li * dt)
    den = lr * lr + li * li
    nr = ar - 1.0
    zr = (nr * lr + ai * li) / den
    zi = (ai * lr - nr * li) / den
    br_, bi_ = b_re.astype(f32), b_im.astype(f32)
    bbr = zr[..., None] * br_ - zi[..., None] * bi_
    bbi = zr[..., None] * bi_ + zi[..., None] * br_
    pr = [jnp.ones_like(ar)]
    pi = [jnp.zeros_like(ai)]
    for _ in range(t):
        r0, i0 = pr[-1], pi[-1]
        pr.append(r0 * ar - i0 * ai)
        pi.append(r0 * ai + i0 * ar)
    pwr = jnp.stack(pr, axis=1)
    pwi = jnp.stack(pi, axis=1)
    cr, ci = c_re.astype(f32), c_im.astype(f32)
    car = cr[:, None] * pwr[:, :t, None, :] - ci[:, None] * pwi[:, :t, None, :]
    cai = cr[:, None] * pwi[:, :t, None, :] + ci[:, None] * pwr[:, :t, None, :]
    kk = (jnp.einsum('gtnp,gpm->gtnm', car, bbr, precision=hp)
          - jnp.einsum('gtnp,gpm->gtnm', cai, bbi, precision=hp))
    ti = np.arange(t)
    lag = ti[None, :] - ti[:, None]
    toep = kk[:, np.clip(lag, 0, t - 1)]
    toep = jnp.where((lag >= 0)[None, :, :, None, None], toep, 0.0)
    sg = S5_SLAB_GROUPS
    ns = B_GROUPS // sg
    ch = B_GROUP_CH
    ns2 = 2 * B_STATE
    grp = np.arange(LANES) // ch

    def spread(compact, place, keep):
        full = jnp.einsum('srk,kc->src', _bf(compact), jnp.asarray(place, jnp.bfloat16),
                          preferred_element_type=jnp.bfloat16)
        return jnp.where(jnp.asarray(keep), full, jnp.zeros((), jnp.bfloat16))

    col_un = np.kron(np.eye(t), np.kron(np.ones((1, sg)), np.eye(ch)))
    col_p = np.kron(np.ones((1, sg)), np.eye(ns2))
    row_g = np.tile(grp, t)
    toep = jnp.transpose(toep.reshape(ns, sg, t, t, ch, ch), (0, 2, 1, 5, 3, 4))
    toep = spread(toep.reshape(ns, t * LANES, t * ch), col_un,
                  row_g[:, None] == np.tile(grp, t)[None, :])
    er = pwr[:, t - 1 - ti][:, :, None, :]
    ei = pwi[:, t - 1 - ti][:, :, None, :]
    bbr_t = jnp.transpose(bbr, (0, 2, 1))[:, None]
    bbi_t = jnp.transpose(bbi, (0, 2, 1))[:, None]
    w1 = jnp.concatenate([er * bbr_t - ei * bbi_t, er * bbi_t + ei * bbr_t], axis=-1)
    w1 = jnp.transpose(w1.reshape(ns, sg, t, ch, ns2), (0, 2, 1, 3, 4))
    w1 = spread(w1.reshape(ns, t * LANES, ns2), col_p,
                row_g[:, None] == np.repeat(np.arange(sg), ns2)[None, :])
    zr1 = pwr[:, 1:, None, :]
    zi1 = pwi[:, 1:, None, :]
    w2r = cr[:, None] * zr1 - ci[:, None] * zi1
    w2i = -(cr[:, None] * zi1 + ci[:, None] * zr1)
    w2 = jnp.concatenate([w2r, w2i], axis=-1)
    w2 = jnp.transpose(w2.reshape(ns, sg, t, ch, ns2), (0, 1, 4, 2, 3))
    w2 = spread(w2.reshape(ns, sg * ns2, t * ch), col_un,
                np.repeat(np.arange(sg), ns2)[:, None] == np.tile(grp, t)[None, :])
    a1 = jnp.concatenate([pwr[:, t], pwr[:, t]], axis=-1)
    a2 = jnp.concatenate([-pwi[:, t], pwi[:, t]], axis=-1)
    dd = jnp.tile(d.astype(f32).reshape(ns, 1, LANES), (1, 1, t))
    return toep, w1, w2, a1, a2, dd


def _s5_rows(u_ref, nc):
    return jnp.concatenate([u_ref[pl.ds(t, nc, stride=S5_CHUNK), :] for t in range(S5_CHUNK)],
                           axis=1)


def _s5_local_kernel(u_ref, w1_ref, e_ref):
    e_ref[...] = _dot(_bf(_s5_rows(u_ref, e_ref.shape[0])), w1_ref[0])


def _s5_scan_kernel(e_ref, a1_ref, a2_ref, xp_ref, st_ref):
    @pl.when(pl.program_id(0) == 0)
    def _():
        st_ref[...] = jnp.zeros_like(st_ref)

    a1 = a1_ref[...]
    a2 = a2_ref[...]

    def step(c, carry):
        x = st_ref[...]
        xp_ref[c] = x
        st_ref[...] = a1 * x + a2 * pltpu.roll(x, B_STATE, 1) + e_ref[c]
        return carry

    lax.fori_loop(0, e_ref.shape[0], step, 0)


def _s5_out_kernel(u_ref, xp_ref, toep_ref, w2_ref, d_ref, y_ref):
    nc = xp_ref.shape[0]
    x = _s5_rows(u_ref, nc)
    y = _dot(_bf(x), toep_ref[0]) + _dot(_bf(xp_ref[...]), w2_ref[0]) + d_ref[0] * x
    y = _gelu(y)
    for t in range(S5_CHUNK):
        y_ref[pl.ds(t, nc, stride=S5_CHUNK), :] = y[:, t * LANES:(t + 1) * LANES]


def _s5(proj, tables, cb, nrb):
    toep, w1, w2, a1, a2, dd = tables
    n_tok = proj.shape[0]
    nc = n_tok // S5_CHUNK
    g = B_GROUPS
    ns = 2 * B_STATE
    n_slab = B_WIDTH // LANES
    sw = S5_SLAB_GROUPS * ns
    u0 = 4 * A_WIDTH // LANES
    cb = min(cb, nc)
    ncb = nc // nrb
    tokb = n_tok // nrb
    e = pl.pallas_call(
        _s5_local_kernel,
        grid=(n_slab, nrb),
        in_specs=[pl.BlockSpec((tokb, LANES), lambda s, r: (r, u0 + s)),
                  pl.BlockSpec((1, S5_CHUNK * LANES, sw), lambda s, r: (s, 0, 0))],
        out_specs=pl.BlockSpec((ncb, sw), lambda s, r: (r, s)),
        out_shape=jax.ShapeDtypeStruct((nc, g * ns), jnp.float32),
        compiler_params=_cparams("parallel", "parallel"),
        name="s5_local",
    )(proj, w1)
    xp = pl.pallas_call(
        _s5_scan_kernel,
        grid=(nc // cb,),
        in_specs=[pl.BlockSpec((cb, g, ns), lambda i: (i, 0, 0)),
                  pl.BlockSpec((g, ns), lambda i: (0, 0)),
                  pl.BlockSpec((g, ns), lambda i: (0, 0))],
        out_specs=pl.BlockSpec((cb, g, ns), lambda i: (i, 0, 0)),
        out_shape=jax.ShapeDtypeStruct((nc, g, ns), jnp.float32),
        scratch_shapes=[pltpu.VMEM((g, ns), jnp.float32)],
        compiler_params=_cparams("arbitrary"),
        name="s5_scan",
    )(e.reshape(nc, g, ns), a1, a2)
    return pl.pallas_call(
        _s5_out_kernel,
        grid=(n_slab, nrb),
        in_specs=[pl.BlockSpec((tokb, LANES), lambda s, r: (r, u0 + s)),
                  pl.BlockSpec((ncb, sw), lambda s, r: (r, s)),
                  pl.BlockSpec((1, S5_CHUNK * LANES, S5_CHUNK * LANES), lambda s, r: (s, 0, 0)),
                  pl.BlockSpec((1, sw, S5_CHUNK * LANES), lambda s, r: (s, 0, 0)),
                  pl.BlockSpec((1, 1, S5_CHUNK * LANES), lambda s, r: (s, 0, 0))],
        out_specs=pl.BlockSpec((tokb, LANES), lambda s, r: (r, s)),
        out_shape=jax.ShapeDtypeStruct((n_tok, B_WIDTH), jnp.float32),
        compiler_params=_cparams("parallel", "parallel"),
        name="s5_out",
    )(proj, xp.reshape(nc, g * ns), toep, w2, dd)


def _glu_kernel(y_ref, wa_ref, wb_ref, o_ref):
    y = _bf(y_ref[...])
    o_ref[...] = (_dot(y, wa_ref[...]) * _sigmoid(_dot(y, wb_ref[...]))).astype(o_ref.dtype)


def _glu(y, w, layer, tm, tn):
    m, k = y.shape
    n = w.shape[2] // 2
    tm = min(tm, m)
    nb = n // tn
    return pl.pallas_call(
        _glu_kernel,
        grid=(nb, m // tm),
        in_specs=[pl.BlockSpec((tm, k), lambda j, i: (i, 0)),
                  pl.BlockSpec((None, k, tn), lambda j, i: (layer, 0, j)),
                  pl.BlockSpec((None, k, tn), lambda j, i: (layer, 0, nb + j))],
        out_specs=pl.BlockSpec((tm, tn), lambda j, i: (i, j)),
        out_shape=jax.ShapeDtypeStruct((m, n), jnp.bfloat16),
        compiler_params=_cparams("parallel", "parallel"),
        name="glu",
    )(y, w, w)


def _up_merge_kernel(oa_ref, ob_ref, wa_ref, wb_ref, ga_ref, gb_ref, o_ref):
    m = (_sigmoid(ga_ref[...]) * _dot(oa_ref[...], wa_ref[...])
         + _sigmoid(gb_ref[...]) * _dot(ob_ref[...], wb_ref[...]))
    o_ref[...] = m.astype(o_ref.dtype)


def _up_merge(oa, ob, wa, wb, layer, proj, tm, tn):
    m, k = oa.shape
    n = wa.shape[2]
    tm = min(tm, m)
    ga0 = (4 * A_WIDTH + B_WIDTH) // tn
    gb0 = ga0 + D_MODEL // tn
    return pl.pallas_call(
        _up_merge_kernel,
        grid=(n // tn, m // tm),
        in_specs=[pl.BlockSpec((tm, k), lambda j, i: (i, 0)),
                  pl.BlockSpec((tm, k), lambda j, i: (i, 0)),
                  pl.BlockSpec((None, k, tn), lambda j, i: (layer, 0, j)),
                  pl.BlockSpec((None, k, tn), lambda j, i: (layer, 0, j)),
                  pl.BlockSpec((tm, tn), lambda j, i: (i, ga0 + j)),
                  pl.BlockSpec((tm, tn), lambda j, i: (i, gb0 + j))],
        out_specs=pl.BlockSpec((tm, tn), lambda j, i: (i, j)),
        out_shape=jax.ShapeDtypeStruct((m, n), jnp.bfloat16),
        compiler_params=_cparams("parallel", "parallel"),
        name="up_merge",
    )(oa, ob, wa, wb, proj, proj)


def _wo_ln_kernel(m_ref, w_ref, x_ref, g_ref, b_ref, o_ref, ob_ref):
    z = ALPHA * x_ref[...] + _dot(m_ref[...], w_ref[...])
    y = _layer_norm(z, g_ref[...], b_ref[...])
    o_ref[...] = y
    ob_ref[...] = _bf(y)


def _wo_ln(merged, w, layer, x, g, b, tm):
    m, k = merged.shape
    n = w.shape[2]
    tm = min(tm, m)
    row = lambda i: (i, 0)
    fix = lambda i: (0, 0)
    return pl.pallas_call(
        _wo_ln_kernel,
        grid=(m // tm,),
        in_specs=[pl.BlockSpec((tm, k), row), pl.BlockSpec((None, k, n), lambda i: (layer, 0, 0)),
                  pl.BlockSpec((tm, n), row), pl.BlockSpec((1, n), fix),
                  pl.BlockSpec((1, n), fix)],
        out_specs=[pl.BlockSpec((tm, n), row), pl.BlockSpec((tm, n), row)],
        out_shape=[jax.ShapeDtypeStruct((m, n), jnp.float32),
                   jax.ShapeDtypeStruct((m, n), jnp.bfloat16)],
        compiler_params=_cparams("parallel"),
        name="wo_ln",
    )(merged, w, x, g, b)


def _add_ln_kernel(x_ref, y_ref, g_ref, b_ref, o_ref, ob_ref):
    y = _layer_norm(ALPHA * x_ref[...] + y_ref[...], g_ref[...], b_ref[...])
    o_ref[...] = y
    ob_ref[...] = _bf(y)


def _add_ln(x, y, g, b, tm):
    m, n = x.shape
    tm = min(tm, m)
    row = lambda i: (i, 0)
    fix = lambda i: (0, 0)
    return pl.pallas_call(
        _add_ln_kernel,
        grid=(m // tm,),
        in_specs=[pl.BlockSpec((tm, n), row), pl.BlockSpec((tm, n), row),
                  pl.BlockSpec((1, n), fix), pl.BlockSpec((1, n), fix)],
        out_specs=[pl.BlockSpec((tm, n), row), pl.BlockSpec((tm, n), row)],
        out_shape=[jax.ShapeDtypeStruct((m, n), jnp.float32),
                   jax.ShapeDtypeStruct((m, n), jnp.bfloat16)],
        compiler_params=_cparams("parallel"),
        name="add_ln",
    )(x, y, g, b)


def _take_top(s, n_take, on_take):
    rows = s.shape[0]
    rid = lax.broadcasted_iota(jnp.int32, s.shape, 0)
    for k in range(n_take):
        m = jnp.max(s, axis=0, keepdims=True)
        idx = jnp.min(jnp.where(s == m, rid, rows), axis=0, keepdims=True)
        on_take(k, m, idx)
        s = jnp.where(rid == idx, -jnp.inf, s)


def _peer_route_kernel(q_ref, keys_ref, r2_ref, e2_ref, n_ref, c_ref):
    tt = q_ref.shape[0]
    kid = lax.broadcasted_iota(jnp.int32, (P_NKEYS, tt), 0)
    rank_id = lax.broadcasted_iota(jnp.int32, (P_TOPK, tt), 0)
    for h in range(P_HEADS):
        s = []
        for half in range(2):
            qh = q_ref[:, (2 * h + half) * P_HALF:(2 * h + half + 1) * P_HALF]
            s.append(lax.dot_general(_bf(keys_ref[h, half]), _bf(qh), _NT,
                                     preferred_element_type=jnp.float32))
        vals = [[], []]
        idxs = [[], []]
        rank2 = [jnp.full((P_NKEYS, tt), P_TOPK, jnp.int32)]
        for half in range(2):
            def take(k, m, idx, half=half):
                vals[half].append(m)
                idxs[half].append(idx)
                if half == 1:
                    rank2[0] = jnp.where(kid == idx, k, rank2[0])
            _take_top(s[half], P_TOPK, take)
        v1 = jnp.concatenate(vals[0], axis=0)
        v2 = jnp.concatenate(vals[1], axis=0)
        widths = [P_TOPK // (i + 1) for i in range(P_TOPK)]
        starts = np.cumsum([0] + widths)
        pad = int(-starts[-1] % 8)
        cand = jnp.concatenate([v1[i:i + 1] + v2[:widths[i]] for i in range(P_TOPK)]
                               + [jnp.full((pad, tt), -jnp.inf, jnp.float32)], axis=0)
        state = [jnp.zeros((P_TOPK, tt), jnp.int32), jnp.zeros((1, tt), jnp.float32)]
        top = v1[0:1] + v2[0:1]

        def take_c(k, m, pos):
            rank1 = sum((pos >= int(st)).astype(jnp.int32) for st in starts[1:P_TOPK])
            state[0] = state[0] + (rank_id == rank1).astype(jnp.int32)
            state[1] = state[1] + jnp.exp(m - top)
        _take_top(cand, P_TOPK, take_c)
        n_rank, z = state
        n_key = jnp.zeros((P_NKEYS, tt), jnp.int32)
        for i in range(P_TOPK):
            n_key = jnp.where(kid == idxs[0][i], n_rank[i:i + 1], n_key)
        r2_ref[h] = rank2[0].astype(jnp.float32)
        n_ref[h] = n_key.astype(jnp.float32)
        e2_ref[h] = jnp.exp(s[1] - v2[0:1])
        c_ref[h] = jnp.exp(s[0] - v1[0:1]) / z


def _peer_route(q, keys, tt):
    n_tok = q.shape[0]
    tt = min(tt, n_tok)
    out = jax.ShapeDtypeStruct((P_HEADS, P_NKEYS, n_tok), jnp.float32)
    ospec = pl.BlockSpec((P_HEADS, P_NKEYS, tt), lambda i: (0, 0, i))
    return pl.pallas_call(
        _peer_route_kernel,
        grid=(n_tok // tt,),
        in_specs=[pl.BlockSpec((tt, P_HEADS * P_QDIM), lambda i: (i, 0)),
                  pl.BlockSpec((P_HEADS, 2, P_NKEYS, P_HALF), lambda i: (0, 0, 0, 0))],
        out_specs=[ospec] * 4,
        out_shape=[out] * 4,
        compiler_params=_cparams("parallel"),
        name="peer_route",
    )(q, keys)


def _peer_ffn_kernel(xt_ref, u_ref, vt_ref, r2_ref, e2_ref, n_ref, c_ref, y_ref, acc_ref, h_ref):
    j = pl.program_id(1)
    eb = u_ref.shape[0]
    tm = xt_ref.shape[1]

    @pl.when(j == 0)
    def _():
        acc_ref[...] = jnp.zeros_like(acc_ref)

    act = _dot(u_ref[...], xt_ref[...])
    for al in range(eb // P_NKEYS):
        a = j * (eb // P_NKEYS) + al
        rows = slice(al * P_NKEYS, (al + 1) * P_NKEYS)
        for cc in range(tm // GATE_COLS):
            cols = slice(cc * GATE_COLS, (cc + 1) * GATE_COLS)
            gate = None
            for h in range(P_HEADS):
                n_row = n_ref[h, pl.ds(a, 1), :][:, cols]
                c_row = c_ref[h, pl.ds(a, 1), :][:, cols]
                term = jnp.where(r2_ref[h, :, cols] < n_row, e2_ref[h, :, cols] * c_row, 0.0)
                gate = term if gate is None else gate + term
            h_ref[rows, cols] = (_gelu(act[rows, cols]) * gate).astype(h_ref.dtype)
    acc_ref[...] += _dot(vt_ref[...], h_ref[...])

    @pl.when(j == pl.num_programs(1) - 1)
    def _():
        y_ref[...] = acc_ref[...].T


def _peer_ffn(xt, u, vt, layer, route, tm, eb):
    d, n_tok = xt.shape
    tm = min(tm, n_tok)
    rspec = pl.BlockSpec((P_HEADS, P_NKEYS, tm), lambda i, j: (0, 0, i))
    return pl.pallas_call(
        _peer_ffn_kernel,
        grid=(n_tok // tm, P_NEXP // eb),
        in_specs=[pl.BlockSpec((d, tm), lambda i, j: (0, i)),
                  pl.BlockSpec((None, eb, d), lambda i, j: (layer, j, 0)),
                  pl.BlockSpec((None, d, eb), lambda i, j: (layer, 0, j)),
                  rspec, rspec, rspec, rspec],
        out_specs=pl.BlockSpec((tm, d), lambda i, j: (i, 0)),
        out_shape=jax.ShapeDtypeStruct((n_tok, d), jnp.float32),
        scratch_shapes=[pltpu.VMEM((d, tm), jnp.float32),
                        pltpu.VMEM((eb, tm), u.dtype)],
        compiler_params=_cparams("parallel", "arbitrary"),
        name="peer_ffn",
    )(xt, u, vt, *route)


def _chunk_tri():
    i = np.arange(LANES)
    same = (i[:, None] // HG_CHUNK) == (i[None, :] // HG_CHUNK)
    return jnp.asarray(same & (i[None, :] <= i[:, None]), jnp.bfloat16)


def kernel(x, w_in, hgrn_lb_logits, hgrn_norm_g, s5_lambda_re, s5_lambda_im, s5_log_step,
           s5_b_re, s5_b_im, s5_c_re, s5_c_im, s5_d, s5_w_glu, w_up_a, w_up_b, w_o,
           ln1_g, ln1_b, peer_w_q, peer_keys, peer_u, peer_v, ln2_g, ln2_b):
    bsz, n_tok, d = x.shape
    assert bsz == 1 and d == D_MODEL and n_tok % (S5_CHUNK * 8) == 0
    f32 = jnp.float32
    p = jax.nn.softmax(hgrn_lb_logits.astype(f32), axis=0)
    c = jnp.cumsum(p, axis=0)
    lbs = c - c[0:1]
    loglb = jnp.log(lbs)
    log1mlb = jnp.log1p(-lbs)
    tri = _chunk_tri()

    w_in_b, w_glu_b, w_ua_b, w_ub_b, w_o_b, w_q_b, u_b = (
        _bf(w) for w in (w_in, s5_w_glu, w_up_a, w_up_b, w_o, peer_w_q, peer_u))
    vt_b = _bf(jnp.swapaxes(peer_v, 1, 2))
    row = lambda p, l: p[l][None].astype(f32)

    xf = x.reshape(n_tok, d).astype(f32)
    xb = _bf(xf)
    for l in range(DEPTH):
        proj = _matmul(xb, w_in_b, l, f32, tm=512, tn=1024)
        oa = _hgrn2(proj, loglb[l:l + 1], log1mlb[l:l + 1], row(hgrn_norm_g, l), tri, tb=512)
        tables = _s5_tables(s5_lambda_re[l], s5_lambda_im[l], s5_log_step[l], s5_b_re[l],
                            s5_b_im[l], s5_c_re[l], s5_c_im[l], s5_d[l])
        yb = _s5(proj, tables, cb=64, nrb=2)
        ob = _glu(yb, w_glu_b, l, tm=1024, tn=512)
        merged = _up_merge(oa, ob, w_ua_b, w_ub_b, l, proj, tm=1024, tn=512)
        xf, xb = _wo_ln(merged, w_o_b, l, xf, row(ln1_g, l), row(ln1_b, l), tm=256)
        q = _matmul(xb, w_q_b, l, f32, tm=512, tn=1024)
        route = _peer_route(q, peer_keys[l].astype(f32), tt=128)
        y = _peer_ffn(xb.T, u_b, vt_b, l, route, tm=512, eb=512)
        xf, xb = _add_ln(xf, y, row(ln2_g, l), row(ln2_b, l), tm=256)
    return xf.reshape(bsz, n_tok, d).astype(x.dtype)
```

```python
import math

import jax
import jax.numpy as jnp
import numpy as np
from jax import lax
from jax.experimental import pallas as pl
from jax.experimental.pallas import tpu as pltpu

D_MODEL = 2048
DEPTH = 4
A_HEADS = 8
A_DIM = 128
A_WIDTH = A_HEADS * A_DIM
HG_CHUNK = 16
HG_UNROLL = 8
B_GROUPS = 64
B_GROUP_CH = 16
B_STATE = 64
B_WIDTH = B_GROUPS * B_GROUP_CH
S5_CHUNK = 16
S5_SLAB_GROUPS = 8
S5_EIG_CLIP = -1e-4
IN_COLS = 4 * A_WIDTH + B_WIDTH + 2 * D_MODEL
P_HEADS = 8
P_QDIM = 256
P_HALF = 128
P_NKEYS = 128
P_NEXP = P_NKEYS * P_NKEYS
P_TOPK = 16
ALPHA = (2.0 * DEPTH) ** 0.25
LN_EPS = 1e-5
RMS_EPS = 1e-6

LANES = 128
GATE_COLS = 128
VMEM_LIMIT = 52 * 1024 * 1024

_NT = (((1,), (1,)), ((), ()))
_TN = (((0,), (0,)), ((), ()))


def _cparams(*sem):
    return pltpu.CompilerParams(dimension_semantics=sem, vmem_limit_bytes=VMEM_LIMIT)


def _bf(x):
    return x.astype(jnp.bfloat16)


def _dot(a, b):
    return jnp.dot(a, b, preferred_element_type=jnp.float32)


def _sigmoid(x):
    return 1.0 / (1.0 + jnp.exp(-x))


def _gelu(x):
    c = math.sqrt(2.0 / math.pi)
    return 0.5 * x * (1.0 + jnp.tanh(c * (x + 0.044715 * (x * x * x))))


def _layer_norm(z, g, b):
    mu = jnp.mean(z, axis=-1, keepdims=True)
    zc = z - mu
    var = jnp.mean(zc * zc, axis=-1, keepdims=True)
    return zc * lax.rsqrt(var + LN_EPS) * g + b


def _matmul_kernel(a_ref, w_ref, o_ref):
    o_ref[...] = _dot(a_ref[...], w_ref[...]).astype(o_ref.dtype)


def _matmul(a, w, layer, out_dtype, tm, tn):
    m, k = a.shape
    n = w.shape[2]
    tm = min(tm, m)
    return pl.pallas_call(
        _matmul_kernel,
        grid=(n // tn, m // tm),
        in_specs=[pl.BlockSpec((tm, k), lambda j, i: (i, 0)),
                  pl.BlockSpec((None, k, tn), lambda j, i: (layer, 0, j))],
        out_specs=pl.BlockSpec((tm, tn), lambda j, i: (i, j)),
        out_shape=jax.ShapeDtypeStruct((m, n), out_dtype),
        compiler_params=_cparams("parallel", "parallel"),
        name="matmul",
    )(a, w)


def _hgrn2_kernel(q_ref, f_ref, i_ref, g_ref, loglb_ref, log1mlb_ref, ng_ref, tri_ref,
                  o_ref, st_ref, b_ref, k_ref):
    tb = q_ref.shape[0]

    @pl.when(pl.program_id(1) == 0)
    def _():
        st_ref[...] = jnp.zeros_like(st_ref)

    fz = f_ref[...]
    log_sig = jnp.minimum(fz, 0.0) - jnp.log1p(jnp.exp(-jnp.abs(fz)))
    z = log1mlb_ref[...] + log_sig
    a = loglb_ref[...]
    logf = jnp.maximum(a, z) + jnp.log1p(jnp.exp(-jnp.abs(a - z)))
    k_ref[...] = 1.0 - jnp.exp(logf)
    tri = tri_ref[...]
    for r in range(tb // LANES):
        rows = slice(r * LANES, (r + 1) * LANES)
        lf = logf[rows]
        hi = _bf(lf)
        r1 = lf - hi.astype(jnp.float32)
        mid = _bf(r1)
        lo = _bf(r1 - mid.astype(jnp.float32))
        b_ref[rows, :] = _dot(tri, hi) + _dot(tri, mid) + _dot(tri, lo)

    srow = lax.broadcasted_iota(jnp.int32, (HG_CHUNK, 1), 0)
    ng = ng_ref[...]

    def chunk(j, st):
        rows = pl.ds(pl.multiple_of(j * HG_CHUNK, HG_CHUNK), HG_CHUNK)
        b = b_ref[rows, :]
        q = q_ref[rows, :]
        k = k_ref[rows, :]
        v = i_ref[rows, :]
        b_last = b[HG_CHUNK - 1:HG_CHUNK, :]
        o = lax.dot_general(_bf(q * jnp.exp(b)), _bf(st), _NT,
                            preferred_element_type=jnp.float32)
        intra = []
        for t in range(HG_CHUNK):
            diff = jnp.where(srow <= t, b[t:t + 1, :] - b, -jnp.inf)
            p = (q[t:t + 1, :] * k) * jnp.exp(diff)
            att = jnp.sum(p, axis=-1, keepdims=True)
            intra.append(jnp.sum(att * v, axis=0, keepdims=True))
        o = o + jnp.concatenate(intra, axis=0)
        k_tail = k * jnp.exp(b_last - b)
        st = st * jnp.exp(b_last) + lax.dot_general(
            _bf(v), _bf(k_tail), _TN, preferred_element_type=jnp.float32)
        o = o * lax.rsqrt(jnp.mean(o * o, axis=-1, keepdims=True) + RMS_EPS)
        o = o * ng * _sigmoid(g_ref[rows, :])
        o_ref[rows, :] = o.astype(o_ref.dtype)
        return st

    st_ref[...] = lax.fori_loop(0, tb // HG_CHUNK, chunk, st_ref[...], unroll=HG_UNROLL)


def _hgrn2(proj, loglb, log1mlb, norm_g, tri, tb):
    n_tok = proj.shape[0]
    tb = min(tb, n_tok)

    def col(off):
        return pl.BlockSpec((tb, A_DIM), lambda h, i: (i, off + h))

    def par():
        return pl.BlockSpec((1, A_DIM), lambda h, i: (0, h))

    return pl.pallas_call(
        _hgrn2_kernel,
        grid=(A_HEADS, n_tok // tb),
        in_specs=[col(0), col(A_HEADS), col(2 * A_HEADS), col(3 * A_HEADS),
                  par(), par(), par(),
                  pl.BlockSpec((LANES, LANES), lambda h, i: (0, 0))],
        out_specs=pl.BlockSpec((tb, A_DIM), lambda h, i: (i, h)),
        out_shape=jax.ShapeDtypeStruct((n_tok, A_WIDTH), jnp.bfloat16),
        scratch_shapes=[pltpu.VMEM((A_DIM, A_DIM), jnp.float32),
                        pltpu.VMEM((tb, A_DIM), jnp.float32),
                        pltpu.VMEM((tb, A_DIM), jnp.float32)],
        compiler_params=_cparams("parallel", "arbitrary"),
        name="hgrn2",
    )(proj, proj, proj, proj, loglb, log1mlb, norm_g, tri)


def _s5_tables(lam_re, lam_im, log_step, b_re, b_im, c_re, c_im, d):
    f32 = jnp.float32
    hp = lax.Precision.HIGHEST
    t = S5_CHUNK
    lr = jnp.minimum(lam_re.astype(f32), S5_EIG_CLIP)
    li = lam_im.astype(f32)
    dt = jnp.exp(log_step.astype(f32))[:, None]
    mag = jnp.exp(lr * dt)
    ar = mag * jnp.cos(li * dt)
    ai = mag * jnp.sin(li * dt)
    den = lr * lr + li * li
    nr = ar - 1.0
    zr = (nr * lr + ai * li) / den
    zi = (ai * lr - nr * li) / den
    br_, bi_ = b_re.astype(f32), b_im.astype(f32)
    bbr = zr[..., None] * br_ - zi[..., None] * bi_
    bbi = zr[..., None] * bi_ + zi[..., None] * br_
    pr = [jnp.ones_like(ar)]
    pi = [jnp.zeros_like(ai)]
    for _ in range(t):
        r0, i0 = pr[-1], pi[-1]
        pr.append(r0 * ar - i0 * ai)
        pi.append(r0 * ai + i0 * ar)
    pwr = jnp.stack(pr, axis=1)
    pwi = jnp.stack(pi, axis=1)
    cr, ci = c_re.astype(f32), c_im.astype(f32)
    car = cr[:, None] * pwr[:, :t, None, :] - ci[:, None] * pwi[:, :t, None, :]
    cai = cr[:, None] * pwi[:, :t, None, :] + ci[:, None] * pwr[:, :t, None, :]
    kk = (jnp.einsum('gtnp,gpm->gtnm', car, bbr, precision=hp)
          - jnp.einsum('gtnp,gpm->gtnm', cai, bbi, precision=hp))
    ti = np.arange(t)
    lag = ti[None, :] - ti[:, None]
    toep = kk[:, np.clip(lag, 0, t - 1)]
    toep = jnp.where((lag >= 0)[None, :, :, None, None], toep, 0.0)
    sg = S5_SLAB_GROUPS
    ns = B_GROUPS // sg
    ch = B_GROUP_CH
    ns2 = 2 * B_STATE
    grp = np.arange(LANES) // ch

    def spread(compact, place, keep):
        full = jnp.einsum('srk,kc->src', _bf(compact), jnp.asarray(place, jnp.bfloat16),
                          preferred_element_type=jnp.bfloat16)
        return jnp.where(jnp.asarray(keep), full, jnp.zeros((), jnp.bfloat16))

    col_un = np.kron(np.eye(t), np.kron(np.ones((1, sg)), np.eye(ch)))
    col_p = np.kron(np.ones((1, sg)), np.eye(ns2))
    row_g = np.tile(grp, t)
    toep = jnp.transpose(toep.reshape(ns, sg, t, t, ch, ch), (0, 2, 1, 5, 3, 4))
    toep = spread(toep.reshape(ns, t * LANES, t * ch), col_un,
                  row_g[:, None] == np.tile(grp, t)[None, :])
    er = pwr[:, t - 1 - ti][:, :, None, :]
    ei = pwi[:, t - 1 - ti][:, :, None, :]
    bbr_t = jnp.transpose(bbr, (0, 2, 1))[:, None]
    bbi_t = jnp.transpose(bbi, (0, 2, 1))[:, None]
    w1 = jnp.concatenate([er * bbr_t - ei * bbi_t, er * bbi_t + ei * bbr_t], axis=-1)
    w1 = jnp.transpose(w1.reshape(ns, sg, t, ch, ns2), (0, 2, 1, 3, 4))
    w1 = spread(w1.reshape(ns, t * LANES, ns2), col_p,
                row_g[:, None] == np.repeat(np.arange(sg), ns2)[None, :])
    zr1 = pwr[:, 1:, None, :]
    zi1 = pwi[:, 1:, None, :]
    w2r = cr[:, None] * zr1 - ci[:, None] * zi1
    w2i = -(cr[:, None] * zi1 + ci[:, None] * zr1)
    w2 = jnp.concatenate([w2r, w2i], axis=-1)
    w2 = jnp.transpose(w2.reshape(ns, sg, t, ch, ns2), (0, 1, 4, 2, 3))
    w2 = spread(w2.reshape(ns, sg * ns2, t * ch), col_un,
                np.repeat(np.arange(sg), ns2)[:, None] == np.tile(grp, t)[None, :])
    a1 = jnp.concatenate([pwr[:, t], pwr[:, t]], axis=-1)
    a2 = jnp.concatenate([-pwi[:, t], pwi[:, t]], axis=-1)
    dd = jnp.tile(d.astype(f32).reshape(ns, 1, LANES), (1, 1, t))
    return toep, w1, w2, a1, a2, dd


def _s5_rows(u_ref, nc):
    return jnp.concatenate([u_ref[pl.ds(t, nc, stride=S5_CHUNK), :] for t in range(S5_CHUNK)],
                           axis=1)


def _s5_local_kernel(u_ref, w1_ref, e_ref):
    e_ref[...] = _dot(_bf(_s5_rows(u_ref, e_ref.shape[0])), w1_ref[0])


def _s5_scan_kernel(e_ref, a1_ref, a2_ref, xp_ref, st_ref):
    @pl.when(pl.program_id(0) == 0)
    def _():
        st_ref[...] = jnp.zeros_like(st_ref)

    a1 = a1_ref[...]
    a2 = a2_ref[...]

    def step(c, carry):
        x = st_ref[...]
        xp_ref[c] = x
        st_ref[...] = a1 * x + a2 * pltpu.roll(x, B_STATE, 1) + e_ref[c]
        return carry

    lax.fori_loop(0, e_ref.shape[0], step, 0)


def _s5_out_kernel(u_ref, xp_ref, toep_ref, w2_ref, d_ref, y_ref):
    nc = xp_ref.shape[0]
    x = _s5_rows(u_ref, nc)
    y = _dot(_bf(x), toep_ref[0]) + _dot(_bf(xp_ref[...]), w2_ref[0]) + d_ref[0] * x
    y = _gelu(y)
    for t in range(S5_CHUNK):
        y_ref[pl.ds(t, nc, stride=S5_CHUNK), :] = y[:, t * LANES:(t + 1) * LANES]


def _s5(proj, tables, cb, nrb):
    toep, w1, w2, a1, a2, dd = tables
    n_tok = proj.shape[0]
    nc = n_tok // S5_CHUNK
    g = B_GROUPS
    ns = 2 * B_STATE
    n_slab = B_WIDTH // LANES
    sw = S5_SLAB_GROUPS * ns
    u0 = 4 * A_WIDTH // LANES
    cb = min(cb, nc)
    ncb = nc // nrb
    tokb = n_tok // nrb
    e = pl.pallas_call(
        _s5_local_kernel,
        grid=(n_slab, nrb),
        in_specs=[pl.BlockSpec((tokb, LANES), lambda s, r: (r, u0 + s)),
                  pl.BlockSpec((1, S5_CHUNK * LANES, sw), lambda s, r: (s, 0, 0))],
        out_specs=pl.BlockSpec((ncb, sw), lambda s, r: (r, s)),
        out_shape=jax.ShapeDtypeStruct((nc, g * ns), jnp.float32),
        compiler_params=_cparams("parallel", "parallel"),
        name="s5_local",
    )(proj, w1)
    xp = pl.pallas_call(
        _s5_scan_kernel,
        grid=(nc // cb,),
        in_specs=[pl.BlockSpec((cb, g, ns), lambda i: (i, 0, 0)),
                  pl.BlockSpec((g, ns), lambda i: (0, 0)),
                  pl.BlockSpec((g, ns), lambda i: (0, 0))],
        out_specs=pl.BlockSpec((cb, g, ns), lambda i: (i, 0, 0)),
        out_shape=jax.ShapeDtypeStruct((nc, g, ns), jnp.float32),
        scratch_shapes=[pltpu.VMEM((g, ns), jnp.float32)],
        compiler_params=_cparams("arbitrary"),
        name="s5_scan",
    )(e.reshape(nc, g, ns), a1, a2)
    return pl.pallas_call(
        _s5_out_kernel,
        grid=(n_slab, nrb),
        in_specs=[pl.BlockSpec((tokb, LANES), lambda s, r: (r, u0 + s)),
                  pl.BlockSpec((ncb, sw), lambda s, r: (r, s)),
                  pl.BlockSpec((1, S5_CHUNK * LANES, S5_CHUNK * LANES), lambda s, r: (s, 0, 0)),
                  pl.BlockSpec((1, sw, S5_CHUNK * LANES), lambda s, r: (s, 0, 0)),
                  pl.BlockSpec((1, 1, S5_CHUNK * LANES), lambda s, r: (s, 0, 0))],
        out_specs=pl.BlockSpec((tokb, LANES), lambda s, r: (r, s)),
        out_shape=jax.ShapeDtypeStruct((n_tok, B_WIDTH), jnp.float32),
        compiler_params=_cparams("parallel", "parallel"),
        name="s5_out",
    )(proj, xp.reshape(nc, g * ns), toep, w2, dd)


def _glu_kernel(y_ref, wa_ref, wb_ref, o_ref):
    y = _bf(y_ref[...])
    o_ref[...] = (_dot(y, wa_ref[...]) * _sigmoid(_dot(y, wb_ref[...]))).astype(o_ref.dtype)


def _glu(y, w, layer, tm, tn):
    m, k = y.shape
    n = w.shape[2] // 2
    tm = min(tm, m)
    nb = n // tn
    return pl.pallas_call(
        _glu_kernel,
        grid=(nb, m // tm),
        in_specs=[pl.BlockSpec((tm, k), lambda j, i: (i, 0)),
                  pl.BlockSpec((None, k, tn), lambda j, i: (layer, 0, j)),
                  pl.BlockSpec((None, k, tn), lambda j, i: (layer, 0, nb + j))],
        out_specs=pl.BlockSpec((tm, tn), lambda j, i: (i, j)),
        out_shape=jax.ShapeDtypeStruct((m, n), jnp.bfloat16),
        compiler_params=_cparams("parallel", "parallel"),
        name="glu",
    )(y, w, w)


def _up_merge_kernel(oa_ref, ob_ref, wa_ref, wb_ref, ga_ref, gb_ref, o_ref):
    m = (_sigmoid(ga_ref[...]) * _dot(oa_ref[...], wa_ref[...])
         + _sigmoid(gb_ref[...]) * _dot(ob_ref[...], wb_ref[...]))
    o_ref[...] = m.astype(o_ref.dtype)


def _up_merge(oa, ob, wa, wb, layer, proj, tm, tn):
    m, k = oa.shape
    n = wa.shape[2]
    tm = min(tm, m)
    ga0 = (4 * A_WIDTH + B_WIDTH) // tn
    gb0 = ga0 + D_MODEL // tn
    return pl.pallas_call(
        _up_merge_kernel,
        grid=(n // tn, m // tm),
        in_specs=[pl.BlockSpec((tm, k), lambda j, i: (i, 0)),
                  pl.BlockSpec((tm, k), lambda j, i: (i, 0)),
                  pl.BlockSpec((None, k, tn), lambda j, i: (layer, 0, j)),
                  pl.BlockSpec((None, k, tn), lambda j, i: (layer, 0, j)),
                  pl.BlockSpec((tm, tn), lambda j, i: (i, ga0 + j)),
                  pl.BlockSpec((tm, tn), lambda j, i: (i, gb0 + j))],
        out_specs=pl.BlockSpec((tm, tn), lambda j, i: (i, j)),
        out_shape=jax.ShapeDtypeStruct((m, n), jnp.bfloat16),
        compiler_params=_cparams("parallel", "parallel"),
        name="up_merge",
    )(oa, ob, wa, wb, proj, proj)


def _wo_ln_kernel(m_ref, w_ref, x_ref, g_ref, b_ref, o_ref, ob_ref):
    z = ALPHA * x_ref[...] + _dot(m_ref[...], w_ref[...])
    y = _layer_norm(z, g_ref[...], b_ref[...])
    o_ref[...] = y
    ob_ref[...] = _bf(y)


def _wo_ln(merged, w, layer, x, g, b, tm):
    m, k = merged.shape
    n = w.shape[2]
    tm = min(tm, m)
    row = lambda i: (i, 0)
    fix = lambda i: (0, 0)
    return pl.pallas_call(
        _wo_ln_kernel,
        grid=(m // tm,),
        in_specs=[pl.BlockSpec((tm, k), row), pl.BlockSpec((None, k, n), lambda i: (layer, 0, 0)),
                  pl.BlockSpec((tm, n), row), pl.BlockSpec((1, n), fix),
                  pl.BlockSpec((1, n), fix)],
        out_specs=[pl.BlockSpec((tm, n), row), pl.BlockSpec((tm, n), row)],
        out_shape=[jax.ShapeDtypeStruct((m, n), jnp.float32),
                   jax.ShapeDtypeStruct((m, n), jnp.bfloat16)],
        compiler_params=_cparams("parallel"),
        name="wo_ln",
    )(merged, w, x, g, b)


def _add_ln_kernel(x_ref, y_ref, g_ref, b_ref, o_ref, ob_ref):
    y = _layer_norm(ALPHA * x_ref[...] + y_ref[...], g_ref[...], b_ref[...])
    o_ref[...] = y
    ob_ref[...] = _bf(y)


def _add_ln(x, y, g, b, tm):
    m, n = x.shape
    tm = min(tm, m)
    row = lambda i: (i, 0)
    fix = lambda i: (0, 0)
    return pl.pallas_call(
        _add_ln_kernel,
        grid=(m // tm,),
        in_specs=[pl.BlockSpec((tm, n), row), pl.BlockSpec((tm, n), row),
                  pl.BlockSpec((1, n), fix), pl.BlockSpec((1, n), fix)],
        out_specs=[pl.BlockSpec((tm, n), row), pl.BlockSpec((tm, n), row)],
        out_shape=[jax.ShapeDtypeStruct((m, n), jnp.float32),
                   jax.ShapeDtypeStruct((m, n), jnp.bfloat16)],
        compiler_params=_cparams("parallel"),
        name="add_ln",
    )(x, y, g, b)


def _take_top(s, n_take, on_take):
    rows = s.shape[0]
    rid = lax.broadcasted_iota(jnp.int32, s.shape, 0)
    for k in range(n_take):
        m = jnp.max(s, axis=0, keepdims=True)
        idx = jnp.min(jnp.where(s == m, rid, rows), axis=0, keepdims=True)
        on_take(k, m, idx)
        s = jnp.where(rid == idx, -jnp.inf, s)


def _peer_route_kernel(q_ref, keys_ref, r2_ref, e2_ref, n_ref, c_ref):
    tt = q_ref.shape[0]
    kid = lax.broadcasted_iota(jnp.int32, (P_NKEYS, tt), 0)
    rank_id = lax.broadcasted_iota(jnp.int32, (P_TOPK, tt), 0)
    for h in range(P_HEADS):
        s = []
        for half in range(2):
            qh = q_ref[:, (2 * h + half) * P_HALF:(2 * h + half + 1) * P_HALF]
            s.append(lax.dot_general(_bf(keys_ref[h, half]), _bf(qh), _NT,
                                     preferred_element_type=jnp.float32))
        vals = [[], []]
        idxs = [[], []]
        rank2 = [jnp.full((P_NKEYS, tt), P_TOPK, jnp.int32)]
        for half in range(2):
            def take(k, m, idx, half=half):
                vals[half].append(m)
                idxs[half].append(idx)
                if half == 1:
                    rank2[0] = jnp.where(kid == idx, k, rank2[0])
            _take_top(s[half], P_TOPK, take)
        v1 = jnp.concatenate(vals[0], axis=0)
        v2 = jnp.concatenate(vals[1], axis=0)
        widths = [P_TOPK // (i + 1) for i in range(P_TOPK)]
        starts = np.cumsum([0] + widths)
        pad = int(-starts[-1] % 8)
        cand = jnp.concatenate([v1[i:i + 1] + v2[:widths[i]] for i in range(P_TOPK)]
                               + [jnp.full((pad, tt), -jnp.inf, jnp.float32)], axis=0)
        state = [jnp.zeros((P_TOPK, tt), jnp.int32), jnp.zeros((1, tt), jnp.float32)]
        top = v1[0:1] + v2[0:1]

        def take_c(k, m, pos):
            rank1 = sum((pos >= int(st)).astype(jnp.int32) for st in starts[1:P_TOPK])
            state[0] = state[0] + (rank_id == rank1).astype(jnp.int32)
            state[1] = state[1] + jnp.exp(m - top)
        _take_top(cand, P_TOPK, take_c)
        n_rank, z = state
        n_key = jnp.zeros((P_NKEYS, tt), jnp.int32)
        for i in range(P_TOPK):
            n_key = jnp.where(kid == idxs[0][i], n_rank[i:i + 1], n_key)
        r2_ref[h] = rank2[0].astype(jnp.float32).astype(r2_ref.dtype)
        n_ref[h] = n_key.astype(jnp.float32)
        e2_ref[h] = jnp.exp(s[1] - v2[0:1]).astype(e2_ref.dtype)
        c_ref[h] = jnp.exp(s[0] - v1[0:1]) / z


def _peer_route(q, keys, tt, gate_dtype=jnp.bfloat16):
    n_tok = q.shape[0]
    tt = min(tt, n_tok)
    out = [jax.ShapeDtypeStruct((P_HEADS, P_NKEYS, n_tok), dt)
           for dt in (gate_dtype, gate_dtype, jnp.float32, jnp.float32)]
    ospec = pl.BlockSpec((P_HEADS, P_NKEYS, tt), lambda i: (0, 0, i))
    return pl.pallas_call(
        _peer_route_kernel,
        grid=(n_tok // tt,),
        in_specs=[pl.BlockSpec((tt, P_HEADS * P_QDIM), lambda i: (i, 0)),
                  pl.BlockSpec((P_HEADS, 2, P_NKEYS, P_HALF), lambda i: (0, 0, 0, 0))],
        out_specs=[ospec] * 4,
        out_shape=out,
        compiler_params=_cparams("parallel"),
        name="peer_route",
    )(q, keys)


def _peer_ffn_kernel(xt_ref, u_ref, vt_ref, r2_ref, e2_ref, n_ref, c_ref, y_ref, acc_ref, h_ref):
    j = pl.program_id(1)
    eb = u_ref.shape[0]
    tm = xt_ref.shape[1]

    @pl.when(j == 0)
    def _():
        acc_ref[...] = jnp.zeros_like(acc_ref)

    act = _dot(u_ref[...], xt_ref[...])
    gdt = h_ref.dtype
    pack = 8 * 4 // jnp.dtype(gdt).itemsize
    for al in range(eb // P_NKEYS):
        a = j * (eb // P_NKEYS) + al
        rows = slice(al * P_NKEYS, (al + 1) * P_NKEYS)
        for cc in range(tm // GATE_COLS):
            cols = slice(cc * GATE_COLS, (cc + 1) * GATE_COLS)
            gate = None
            for h in range(P_HEADS):
                n_tile = jnp.broadcast_to(n_ref[h, pl.ds(a, 1), :][:, cols], (pack, GATE_COLS)).astype(gdt)
                c_tile = jnp.broadcast_to(c_ref[h, pl.ds(a, 1), :][:, cols], (pack, GATE_COLS)).astype(gdt)
                r2 = r2_ref[h, :, cols].reshape(P_NKEYS // pack, pack, GATE_COLS)
                e2 = e2_ref[h, :, cols].reshape(P_NKEYS // pack, pack, GATE_COLS)
                term = jnp.where(r2 < n_tile[None], e2 * c_tile[None], jnp.zeros((), gdt))
                gate = term if gate is None else gate + term
            g_act = _gelu(act[rows, cols].astype(gdt)).reshape(P_NKEYS // pack, pack, GATE_COLS)
            h_ref[rows, cols] = (g_act * gate).reshape(P_NKEYS, GATE_COLS)
    acc_ref[...] += _dot(vt_ref[...], h_ref[...])

    @pl.when(j == pl.num_programs(1) - 1)
    def _():
        y_ref[...] = acc_ref[...].T


def _peer_ffn(xt, u, vt, layer, route, tm, eb):
    d, n_tok = xt.shape
    tm = min(tm, n_tok)
    rspec = pl.BlockSpec((P_HEADS, P_NKEYS, tm), lambda i, j: (0, 0, i))
    return pl.pallas_call(
        _peer_ffn_kernel,
        grid=(n_tok // tm, P_NEXP // eb),
        in_specs=[pl.BlockSpec((d, tm), lambda i, j: (0, i)),
                  pl.BlockSpec((None, eb, d), lambda i, j: (layer, j, 0)),
                  pl.BlockSpec((None, d, eb), lambda i, j: (layer, 0, j)),
                  rspec, rspec, rspec, rspec],
        out_specs=pl.BlockSpec((tm, d), lambda i, j: (i, 0)),
        out_shape=jax.ShapeDtypeStruct((n_tok, d), jnp.float32),
        scratch_shapes=[pltpu.VMEM((d, tm), jnp.float32),
                        pltpu.VMEM((eb, tm), u.dtype)],
        compiler_params=_cparams("parallel", "arbitrary"),
        name="peer_ffn",
    )(xt, u, vt, *route)


def _chunk_tri():
    i = np.arange(LANES)
    same = (i[:, None] // HG_CHUNK) == (i[None, :] // HG_CHUNK)
    return jnp.asarray(same & (i[None, :] <= i[:, None]), jnp.bfloat16)


def kernel(x, w_in, hgrn_lb_logits, hgrn_norm_g, s5_lambda_re, s5_lambda_im, s5_log_step,
           s5_b_re, s5_b_im, s5_c_re, s5_c_im, s5_d, s5_w_glu, w_up_a, w_up_b, w_o,
           ln1_g, ln1_b, peer_w_q, peer_keys, peer_u, peer_v, ln2_g, ln2_b):
    bsz, n_tok, d = x.shape
    assert bsz == 1 and d == D_MODEL and n_tok % (S5_CHUNK * 8) == 0
    f32 = jnp.float32
    p = jax.nn.softmax(hgrn_lb_logits.astype(f32), axis=0)
    c = jnp.cumsum(p, axis=0)
    lbs = c - c[0:1]
    loglb = jnp.log(lbs)
    log1mlb = jnp.log1p(-lbs)
    tri = _chunk_tri()

    w_in_b, w_glu_b, w_ua_b, w_ub_b, w_o_b, w_q_b, u_b = (
        _bf(w) for w in (w_in, s5_w_glu, w_up_a, w_up_b, w_o, peer_w_q, peer_u))
    vt_b = _bf(jnp.swapaxes(peer_v, 1, 2))
    row = lambda p, l: p[l][None].astype(f32)

    xf = x.reshape(n_tok, d).astype(f32)
    xb = _bf(xf)
    for l in range(DEPTH):
        proj = _matmul(xb, w_in_b, l, f32, tm=512, tn=1024)
        oa = _hgrn2(proj, loglb[l:l + 1], log1mlb[l:l + 1], row(hgrn_norm_g, l), tri, tb=512)
        tables = _s5_tables(s5_lambda_re[l], s5_lambda_im[l], s5_log_step[l], s5_b_re[l],
                            s5_b_im[l], s5_c_re[l], s5_c_im[l], s5_d[l])
        yb = _s5(proj, tables, cb=64, nrb=2)
        ob = _glu(yb, w_glu_b, l, tm=1024, tn=512)
        merged = _up_merge(oa, ob, w_ua_b, w_ub_b, l, proj, tm=1024, tn=512)
        xf, xb = _wo_ln(merged, w_o_b, l, xf, row(ln1_g, l), row(ln1_b, l), tm=256)
        q = _matmul(xb, w_q_b, l, f32, tm=512, tn=1024)
        route = _peer_route(q, peer_keys[l].astype(f32), tt=128)
        y = _peer_ffn(xb.T, u_b, vt_b, l, route, tm=512, eb=1024)
        xf, xb = _add_ln(xf, y, row(ln2_g, l), row(ln2_b, l), tm=256)
    return xf.reshape(bsz, n_tok, d).astype(x.dtype)
```

```python
import math

import jax
import jax.numpy as jnp
import numpy as np
from jax import lax
from jax.experimental import pallas as pl
from jax.experimental.pallas import tpu as pltpu

D_MODEL = 2048
DEPTH = 4
A_HEADS = 8
A_DIM = 128
A_WIDTH = A_HEADS * A_DIM
HG_CHUNK = 16
HG_UNROLL = 8
B_GROUPS = 64
B_GROUP_CH = 16
B_STATE = 64
B_WIDTH = B_GROUPS * B_GROUP_CH
S5_CHUNK = 16
S5_SLAB_GROUPS = 8
S5_EIG_CLIP = -1e-4
IN_COLS = 4 * A_WIDTH + B_WIDTH + 2 * D_MODEL
P_HEADS = 8
P_QDIM = 256
P_HALF = 128
P_NKEYS = 128
P_NEXP = P_NKEYS * P_NKEYS
P_TOPK = 16
ALPHA = (2.0 * DEPTH) ** 0.25
LN_EPS = 1e-5
RMS_EPS = 1e-6

LANES = 128
GATE_COLS = 128
VMEM_LIMIT = 52 * 1024 * 1024

MM_ROWS, MM_COLS = 1024, 1024
EPI_ROWS, EPI_COLS = 1024, 512
LN_ROWS = 512
HG_ROWS = 1024
S5_SCAN_CHUNKS = 64
S5_ROW_BLOCKS = 2
ROUTE_TOKENS = 128
PEER_TOKENS, PEER_EXPERTS = 512, 1024

_NT = (((1,), (1,)), ((), ()))
_TN = (((0,), (0,)), ((), ()))


def _cparams(*sem):
    return pltpu.CompilerParams(dimension_semantics=sem, vmem_limit_bytes=VMEM_LIMIT)


def _bf(x):
    return x.astype(jnp.bfloat16)


def _dot(a, b):
    return jnp.dot(a, b, preferred_element_type=jnp.float32)


def _sigmoid(x):
    return 1.0 / (1.0 + jnp.exp(-x))


def _gelu(x):
    c = math.sqrt(2.0 / math.pi)
    return 0.5 * x * (1.0 + jnp.tanh(c * (x + 0.044715 * (x * x * x))))


def _layer_norm(z, g, b):
    mu = jnp.mean(z, axis=-1, keepdims=True)
    zc = z - mu
    var = jnp.mean(zc * zc, axis=-1, keepdims=True)
    return zc * lax.rsqrt(var + LN_EPS) * g + b


def _matmul_kernel(a_ref, w_ref, o_ref):
    o_ref[...] = _dot(a_ref[...], w_ref[...]).astype(o_ref.dtype)


def _matmul(a, w, layer, out_dtype, tm, tn):
    m, k = a.shape
    n = w.shape[2]
    tm = min(tm, m)
    return pl.pallas_call(
        _matmul_kernel,
        grid=(n // tn, m // tm),
        in_specs=[pl.BlockSpec((tm, k), lambda j, i: (i, 0)),
                  pl.BlockSpec((None, k, tn), lambda j, i: (layer, 0, j))],
        out_specs=pl.BlockSpec((tm, tn), lambda j, i: (i, j)),
        out_shape=jax.ShapeDtypeStruct((m, n), out_dtype),
        compiler_params=_cparams("parallel", "parallel"),
        name="matmul",
    )(a, w)


def _hgrn2_kernel(q_ref, f_ref, i_ref, g_ref, loglb_ref, log1mlb_ref, ng_ref, tri_ref,
                  o_ref, st_ref, b_ref, k_ref):
    tb = q_ref.shape[0]

    @pl.when(pl.program_id(1) == 0)
    def _():
        st_ref[...] = jnp.zeros_like(st_ref)

    fz = f_ref[...]
    log_sig = jnp.minimum(fz, 0.0) - jnp.log1p(jnp.exp(-jnp.abs(fz)))
    z = log1mlb_ref[...] + log_sig
    a = loglb_ref[...]
    logf = jnp.maximum(a, z) + jnp.log1p(jnp.exp(-jnp.abs(a - z)))
    k_ref[...] = 1.0 - jnp.exp(logf)
    tri = tri_ref[...]
    for r in range(tb // LANES):
        rows = slice(r * LANES, (r + 1) * LANES)
        lf = logf[rows]
        hi = _bf(lf)
        r1 = lf - hi.astype(jnp.float32)
        mid = _bf(r1)
        lo = _bf(r1 - mid.astype(jnp.float32))
        b_ref[rows, :] = _dot(tri, hi) + _dot(tri, mid) + _dot(tri, lo)

    srow = lax.broadcasted_iota(jnp.int32, (HG_CHUNK, 1), 0)
    ng = ng_ref[...]

    def chunk(j, st):
        rows = pl.ds(pl.multiple_of(j * HG_CHUNK, HG_CHUNK), HG_CHUNK)
        b = b_ref[rows, :]
        q = q_ref[rows, :]
        k = k_ref[rows, :]
        v = i_ref[rows, :]
        b_last = b[HG_CHUNK - 1:HG_CHUNK, :]
        o = lax.dot_general(_bf(q * jnp.exp(b)), _bf(st), _NT,
                            preferred_element_type=jnp.float32)
        intra = []
        for t in range(HG_CHUNK):
            diff = jnp.where(srow <= t, b[t:t + 1, :] - b, -jnp.inf)
            p = (q[t:t + 1, :] * k) * jnp.exp(diff)
            att = jnp.sum(p, axis=-1, keepdims=True)
            intra.append(jnp.sum(att * v, axis=0, keepdims=True))
        o = o + jnp.concatenate(intra, axis=0)
        k_tail = k * jnp.exp(b_last - b)
        st = st * jnp.exp(b_last) + lax.dot_general(
            _bf(v), _bf(k_tail), _TN, preferred_element_type=jnp.float32)
        o = o * lax.rsqrt(jnp.mean(o * o, axis=-1, keepdims=True) + RMS_EPS)
        o = o * ng * _sigmoid(g_ref[rows, :])
        o_ref[rows, :] = o.astype(o_ref.dtype)
        return st

    st_ref[...] = lax.fori_loop(0, tb // HG_CHUNK, chunk, st_ref[...], unroll=HG_UNROLL)


def _hgrn2(proj, loglb, log1mlb, norm_g, tri, tb):
    n_tok = proj.shape[0]
    tb = min(tb, n_tok)

    def col(off):
        return pl.BlockSpec((tb, A_DIM), lambda h, i: (i, off + h))

    def par():
        return pl.BlockSpec((1, A_DIM), lambda h, i: (0, h))

    return pl.pallas_call(
        _hgrn2_kernel,
        grid=(A_HEADS, n_tok // tb),
        in_specs=[col(0), col(A_HEADS), col(2 * A_HEADS), col(3 * A_HEADS),
                  par(), par(), par(),
                  pl.BlockSpec((LANES, LANES), lambda h, i: (0, 0))],
        out_specs=pl.BlockSpec((tb, A_DIM), lambda h, i: (i, h)),
        out_shape=jax.ShapeDtypeStruct((n_tok, A_WIDTH), jnp.bfloat16),
        scratch_shapes=[pltpu.VMEM((A_DIM, A_DIM), jnp.float32),
                        pltpu.VMEM((tb, A_DIM), jnp.float32),
                        pltpu.VMEM((tb, A_DIM), jnp.float32)],
        compiler_params=_cparams("parallel", "arbitrary"),
        name="hgrn2",
    )(proj, proj, proj, proj, loglb, log1mlb, norm_g, tri)


def _s5_tables(lam_re, lam_im, log_step, b_re, b_im, c_re, c_im, d):
    f32 = jnp.float32
    hp = lax.Precision.HIGHEST
    t = S5_CHUNK
    lr = jnp.minimum(lam_re.astype(f32), S5_EIG_CLIP)
    li = lam_im.astype(f32)
    dt = jnp.exp(log_step.astype(f32))[:, None]
    mag = jnp.exp(lr * dt)
    ar = mag * jnp.cos(li * dt)
    ai = mag * jnp.sin(li * dt)
    den = lr * lr + li * li
    nr = ar - 1.0
    zr = (nr * lr + ai * li) / den
    zi = (ai * lr - nr * li) / den
    br_, bi_ = b_re.astype(f32), b_im.astype(f32)
    bbr = zr[..., None] * br_ - zi[..., None] * bi_
    bbi = zr[..., None] * bi_ + zi[..., None] * br_
    pr = [jnp.ones_like(ar)]
    pi = [jnp.zeros_like(ai)]
    for _ in range(t):
        r0, i0 = pr[-1], pi[-1]
        pr.append(r0 * ar - i0 * ai)
        pi.append(r0 * ai + i0 * ar)
    pwr = jnp.stack(pr, axis=1)
    pwi = jnp.stack(pi, axis=1)
    cr, ci = c_re.astype(f32), c_im.astype(f32)
    car = cr[:, None] * pwr[:, :t, None, :] - ci[:, None] * pwi[:, :t, None, :]
    cai = cr[:, None] * pwi[:, :t, None, :] + ci[:, None] * pwr[:, :t, None, :]
    kk = (jnp.einsum('gtnp,gpm->gtnm', car, bbr, precision=hp)
          - jnp.einsum('gtnp,gpm->gtnm', cai, bbi, precision=hp))
    ti = np.arange(t)
    lag = ti[None, :] - ti[:, None]
    toep = kk[:, np.clip(lag, 0, t - 1)]
    toep = jnp.where((lag >= 0)[None, :, :, None, None], toep, 0.0)
    sg = S5_SLAB_GROUPS
    ns = B_GROUPS // sg
    ch = B_GROUP_CH
    ns2 = 2 * B_STATE
    grp = np.arange(LANES) // ch

    def spread(compact, place, keep):
        full = jnp.einsum('srk,kc->src', _bf(compact), jnp.asarray(place, jnp.bfloat16),
                          preferred_element_type=jnp.bfloat16)
        return jnp.where(jnp.asarray(keep), full, jnp.zeros((), jnp.bfloat16))

    col_un = np.kron(np.eye(t), np.kron(np.ones((1, sg)), np.eye(ch)))
    col_p = np.kron(np.ones((1, sg)), np.eye(ns2))
    row_g = np.tile(grp, t)
    toep = jnp.transpose(toep.reshape(ns, sg, t, t, ch, ch), (0, 2, 1, 5, 3, 4))
    toep = spread(toep.reshape(ns, t * LANES, t * ch), col_un,
                  row_g[:, None] == np.tile(grp, t)[None, :])
    er = pwr[:, t - 1 - ti][:, :, None, :]
    ei = pwi[:, t - 1 - ti][:, :, None, :]
    bbr_t = jnp.transpose(bbr, (0, 2, 1))[:, None]
    bbi_t = jnp.transpose(bbi, (0, 2, 1))[:, None]
    w1 = jnp.concatenate([er * bbr_t - ei * bbi_t, er * bbi_t + ei * bbr_t], axis=-1)
    w1 = jnp.transpose(w1.reshape(ns, sg, t, ch, ns2), (0, 2, 1, 3, 4))
    w1 = spread(w1.reshape(ns, t * LANES, ns2), col_p,
                row_g[:, None] == np.repeat(np.arange(sg), ns2)[None, :])
    zr1 = pwr[:, 1:, None, :]
    zi1 = pwi[:, 1:, None, :]
    w2r = cr[:, None] * zr1 - ci[:, None] * zi1
    w2i = -(cr[:, None] * zi1 + ci[:, None] * zr1)
    w2 = jnp.concatenate([w2r, w2i], axis=-1)
    w2 = jnp.transpose(w2.reshape(ns, sg, t, ch, ns2), (0, 1, 4, 2, 3))
    w2 = spread(w2.reshape(ns, sg * ns2, t * ch), col_un,
                np.repeat(np.arange(sg), ns2)[:, None] == np.tile(grp, t)[None, :])
    a1 = jnp.concatenate([pwr[:, t], pwr[:, t]], axis=-1)
    a2 = jnp.concatenate([-pwi[:, t], pwi[:, t]], axis=-1)
    dd = jnp.tile(d.astype(f32).reshape(ns, 1, LANES), (1, 1, t))
    return toep, w1, w2, a1, a2, dd


def _s5_rows(u_ref, nc):
    return jnp.concatenate([u_ref[pl.ds(t, nc, stride=S5_CHUNK), :] for t in range(S5_CHUNK)],
                           axis=1)


def _s5_local_kernel(u_ref, w1_ref, e_ref):
    e_ref[...] = _dot(_bf(_s5_rows(u_ref, e_ref.shape[0])), w1_ref[0])


def _s5_scan_kernel(e_ref, a1_ref, a2_ref, xp_ref, st_ref):
    @pl.when(pl.program_id(0) == 0)
    def _():
        st_ref[...] = jnp.zeros_like(st_ref)

    a1 = a1_ref[...]
    a2 = a2_ref[...]

    def step(c, carry):
        x = st_ref[...]
        xp_ref[c] = x
        st_ref[...] = a1 * x + a2 * pltpu.roll(x, B_STATE, 1) + e_ref[c]
        return carry

    lax.fori_loop(0, e_ref.shape[0], step, 0)


def _s5_out_kernel(u_ref, xp_ref, toep_ref, w2_ref, d_ref, y_ref):
    nc = xp_ref.shape[0]
    x = _s5_rows(u_ref, nc)
    y = _dot(_bf(x), toep_ref[0]) + _dot(_bf(xp_ref[...]), w2_ref[0]) + d_ref[0] * x
    y = _gelu(y)
    for t in range(S5_CHUNK):
        y_ref[pl.ds(t, nc, stride=S5_CHUNK), :] = y[:, t * LANES:(t + 1) * LANES]


def _s5(proj, tables, cb, nrb):
    toep, w1, w2, a1, a2, dd = tables
    n_tok = proj.shape[0]
    nc = n_tok // S5_CHUNK
    g = B_GROUPS
    ns = 2 * B_STATE
    n_slab = B_WIDTH // LANES
    sw = S5_SLAB_GROUPS * ns
    u0 = 4 * A_WIDTH // LANES
    cb = min(cb, nc)
    ncb = nc // nrb
    tokb = n_tok // nrb
    e = pl.pallas_call(
        _s5_local_kernel,
        grid=(n_slab, nrb),
        in_specs=[pl.BlockSpec((tokb, LANES), lambda s, r: (r, u0 + s)),
                  pl.BlockSpec((1, S5_CHUNK * LANES, sw), lambda s, r: (s, 0, 0))],
        out_specs=pl.BlockSpec((ncb, sw), lambda s, r: (r, s)),
        out_shape=jax.ShapeDtypeStruct((nc, g * ns), jnp.float32),
        compiler_params=_cparams("parallel", "parallel"),
        name="s5_local",
    )(proj, w1)
    xp = pl.pallas_call(
        _s5_scan_kernel,
        grid=(nc // cb,),
        in_specs=[pl.BlockSpec((cb, g, ns), lambda i: (i, 0, 0)),
                  pl.BlockSpec((g, ns), lambda i: (0, 0)),
                  pl.BlockSpec((g, ns), lambda i: (0, 0))],
        out_specs=pl.BlockSpec((cb, g, ns), lambda i: (i, 0, 0)),
        out_shape=jax.ShapeDtypeStruct((nc, g, ns), jnp.float32),
        scratch_shapes=[pltpu.VMEM((g, ns), jnp.float32)],
        compiler_params=_cparams("arbitrary"),
        name="s5_scan",
    )(e.reshape(nc, g, ns), a1, a2)
    return pl.pallas_call(
        _s5_out_kernel,
        grid=(n_slab, nrb),
        in_specs=[pl.BlockSpec((tokb, LANES), lambda s, r: (r, u0 + s)),
                  pl.BlockSpec((ncb, sw), lambda s, r: (r, s)),
                  pl.BlockSpec((1, S5_CHUNK * LANES, S5_CHUNK * LANES), lambda s, r: (s, 0, 0)),
                  pl.BlockSpec((1, sw, S5_CHUNK * LANES), lambda s, r: (s, 0, 0)),
                  pl.BlockSpec((1, 1, S5_CHUNK * LANES), lambda s, r: (s, 0, 0))],
        out_specs=pl.BlockSpec((tokb, LANES), lambda s, r: (r, s)),
        out_shape=jax.ShapeDtypeStruct((n_tok, B_WIDTH), jnp.float32),
        compiler_params=_cparams("parallel", "parallel"),
        name="s5_out",
    )(proj, xp.reshape(nc, g * ns), toep, w2, dd)


def _glu_kernel(y_ref, wa_ref, wb_ref, o_ref):
    y = _bf(y_ref[...])
    o_ref[...] = (_dot(y, wa_ref[...]) * _sigmoid(_dot(y, wb_ref[...]))).astype(o_ref.dtype)


def _glu(y, w, layer, tm, tn):
    m, k = y.shape
    n = w.shape[2] // 2
    tm = min(tm, m)
    nb = n // tn
    return pl.pallas_call(
        _glu_kernel,
        grid=(nb, m // tm),
        in_specs=[pl.BlockSpec((tm, k), lambda j, i: (i, 0)),
                  pl.BlockSpec((None, k, tn), lambda j, i: (layer, 0, j)),
                  pl.BlockSpec((None, k, tn), lambda j, i: (layer, 0, nb + j))],
        out_specs=pl.BlockSpec((tm, tn), lambda j, i: (i, j)),
        out_shape=jax.ShapeDtypeStruct((m, n), jnp.bfloat16),
        compiler_params=_cparams("parallel", "parallel"),
        name="glu",
    )(y, w, w)


def _up_merge_kernel(oa_ref, ob_ref, wa_ref, wb_ref, ga_ref, gb_ref, o_ref):
    m = (_sigmoid(ga_ref[...]) * _dot(oa_ref[...], wa_ref[...])
         + _sigmoid(gb_ref[...]) * _dot(ob_ref[...], wb_ref[...]))
    o_ref[...] = m.astype(o_ref.dtype)


def _up_merge(oa, ob, wa, wb, layer, proj, tm, tn):
    m, k = oa.shape
    n = wa.shape[2]
    tm = min(tm, m)
    ga0 = (4 * A_WIDTH + B_WIDTH) // tn
    gb0 = ga0 + D_MODEL // tn
    return pl.pallas_call(
        _up_merge_kernel,
        grid=(n // tn, m // tm),
        in_specs=[pl.BlockSpec((tm, k), lambda j, i: (i, 0)),
                  pl.BlockSpec((tm, k), lambda j, i: (i, 0)),
                  pl.BlockSpec((None, k, tn), lambda j, i: (layer, 0, j)),
                  pl.BlockSpec((None, k, tn), lambda j, i: (layer, 0, j)),
                  pl.BlockSpec((tm, tn), lambda j, i: (i, ga0 + j)),
                  pl.BlockSpec((tm, tn), lambda j, i: (i, gb0 + j))],
        out_specs=pl.BlockSpec((tm, tn), lambda j, i: (i, j)),
        out_shape=jax.ShapeDtypeStruct((m, n), jnp.bfloat16),
        compiler_params=_cparams("parallel", "parallel"),
        name="up_merge",
    )(oa, ob, wa, wb, proj, proj)


def _wo_ln_kernel(m_ref, w_ref, x_ref, g_ref, b_ref, o_ref, ob_ref):
    z = ALPHA * x_ref[...] + _dot(m_ref[...], w_ref[...])
    y = _layer_norm(z, g_ref[...], b_ref[...])
    o_ref[...] = y
    ob_ref[...] = _bf(y)


def _wo_ln(merged, w, layer, x, g, b, tm):
    m, k = merged.shape
    n = w.shape[2]
    tm = min(tm, m)
    row = lambda i: (i, 0)
    fix = lambda i: (0, 0)
    return pl.pallas_call(
        _wo_ln_kernel,
        grid=(m // tm,),
        in_specs=[pl.BlockSpec((tm, k), row), pl.BlockSpec((None, k, n), lambda i: (layer, 0, 0)),
                  pl.BlockSpec((tm, n), row), pl.BlockSpec((1, n), fix),
                  pl.BlockSpec((1, n), fix)],
        out_specs=[pl.BlockSpec((tm, n), row), pl.BlockSpec((tm, n), row)],
        out_shape=[jax.ShapeDtypeStruct((m, n), jnp.float32),
                   jax.ShapeDtypeStruct((m, n), jnp.bfloat16)],
        compiler_params=_cparams("parallel"),
        name="wo_ln",
    )(merged, w, x, g, b)


def _add_ln_kernel(x_ref, y_ref, g_ref, b_ref, o_ref, ob_ref):
    y = _layer_norm(ALPHA * x_ref[...] + y_ref[...], g_ref[...], b_ref[...])
    o_ref[...] = y
    ob_ref[...] = _bf(y)


def _add_ln(x, y, g, b, tm):
    m, n = x.shape
    tm = min(tm, m)
    row = lambda i: (i, 0)
    fix = lambda i: (0, 0)
    return pl.pallas_call(
        _add_ln_kernel,
        grid=(m // tm,),
        in_specs=[pl.BlockSpec((tm, n), row), pl.BlockSpec((tm, n), row),
                  pl.BlockSpec((1, n), fix), pl.BlockSpec((1, n), fix)],
        out_specs=[pl.BlockSpec((tm, n), row), pl.BlockSpec((tm, n), row)],
        out_shape=[jax.ShapeDtypeStruct((m, n), jnp.float32),
                   jax.ShapeDtypeStruct((m, n), jnp.bfloat16)],
        compiler_params=_cparams("parallel"),
        name="add_ln",
    )(x, y, g, b)


def _take_top(s, n_take, on_take):
    rows = s.shape[0]
    rid = lax.broadcasted_iota(jnp.int32, s.shape, 0)
    for k in range(n_take):
        m = jnp.max(s, axis=0, keepdims=True)
        idx = jnp.min(jnp.where(s == m, rid, rows), axis=0, keepdims=True)
        on_take(k, m, idx)
        s = jnp.where(rid == idx, -jnp.inf, s)


def _peer_route_kernel(q_ref, keys_ref, r2_ref, e2_ref, n_ref, c_ref):
    tt = q_ref.shape[0]
    kid = lax.broadcasted_iota(jnp.int32, (P_NKEYS, tt), 0)
    rank_id = lax.broadcasted_iota(jnp.int32, (P_TOPK, tt), 0)
    for h in range(P_HEADS):
        s = []
        for half in range(2):
            qh = q_ref[:, (2 * h + half) * P_HALF:(2 * h + half + 1) * P_HALF]
            s.append(lax.dot_general(_bf(keys_ref[h, half]), _bf(qh), _NT,
                                     preferred_element_type=jnp.float32))
        vals = [[], []]
        idxs = [[], []]
        rank2 = [jnp.full((P_NKEYS, tt), P_TOPK, jnp.int32)]
        for half in range(2):
            def take(k, m, idx, half=half):
                vals[half].append(m)
                idxs[half].append(idx)
                if half == 1:
                    rank2[0] = jnp.where(kid == idx, k, rank2[0])
            _take_top(s[half], P_TOPK, take)
        v1 = jnp.concatenate(vals[0], axis=0)
        v2 = jnp.concatenate(vals[1], axis=0)
        widths = [P_TOPK // (i + 1) for i in range(P_TOPK)]
        starts = np.cumsum([0] + widths)
        pad = int(-starts[-1] % 8)
        cand = jnp.concatenate([v1[i:i + 1] + v2[:widths[i]] for i in range(P_TOPK)]
                               + [jnp.full((pad, tt), -jnp.inf, jnp.float32)], axis=0)
        state = [jnp.zeros((P_TOPK, tt), jnp.int32), jnp.zeros((1, tt), jnp.float32)]
        top = v1[0:1] + v2[0:1]

        def take_c(k, m, pos):
            rank1 = sum((pos >= int(st)).astype(jnp.int32) for st in starts[1:P_TOPK])
            state[0] = state[0] + (rank_id == rank1).astype(jnp.int32)
            state[1] = state[1] + jnp.exp(m - top)
        _take_top(cand, P_TOPK, take_c)
        n_rank, z = state
        n_key = jnp.zeros((P_NKEYS, tt), jnp.int32)
        for i in range(P_TOPK):
            n_key = jnp.where(kid == idxs[0][i], n_rank[i:i + 1], n_key)
        r2_ref[h] = rank2[0].astype(jnp.float32).astype(r2_ref.dtype)
        n_ref[h] = n_key.astype(jnp.float32)
        e2_ref[h] = jnp.exp(s[1] - v2[0:1]).astype(e2_ref.dtype)
        c_ref[h] = jnp.exp(s[0] - v1[0:1]) / z


def _peer_route(q, keys, tt, gate_dtype=jnp.bfloat16):
    n_tok = q.shape[0]
    tt = min(tt, n_tok)
    out = [jax.ShapeDtypeStruct((P_HEADS, P_NKEYS, n_tok), dt)
           for dt in (gate_dtype, gate_dtype, jnp.float32, jnp.float32)]
    ospec = pl.BlockSpec((P_HEADS, P_NKEYS, tt), lambda i: (0, 0, i))
    return pl.pallas_call(
        _peer_route_kernel,
        grid=(n_tok // tt,),
        in_specs=[pl.BlockSpec((tt, P_HEADS * P_QDIM), lambda i: (i, 0)),
                  pl.BlockSpec((P_HEADS, 2, P_NKEYS, P_HALF), lambda i: (0, 0, 0, 0))],
        out_specs=[ospec] * 4,
        out_shape=out,
        compiler_params=_cparams("parallel"),
        name="peer_route",
    )(q, keys)


def _peer_ffn_kernel(xt_ref, u_ref, vt_ref, r2_ref, e2_ref, n_ref, c_ref, y_ref, acc_ref, h_ref):
    j = pl.program_id(1)
    eb = u_ref.shape[0]
    tm = xt_ref.shape[1]

    @pl.when(j == 0)
    def _():
        acc_ref[...] = jnp.zeros_like(acc_ref)

    act = _dot(u_ref[...], xt_ref[...])
    gdt = h_ref.dtype
    pack = 8 * 4 // jnp.dtype(gdt).itemsize
    for al in range(eb // P_NKEYS):
        a = j * (eb // P_NKEYS) + al
        rows = slice(al * P_NKEYS, (al + 1) * P_NKEYS)
        for cc in range(tm // GATE_COLS):
            cols = slice(cc * GATE_COLS, (cc + 1) * GATE_COLS)
            gate = None
            for h in range(P_HEADS):
                n_tile = jnp.broadcast_to(n_ref[h, pl.ds(a, 1), :][:, cols], (pack, GATE_COLS)).astype(gdt)
                c_tile = jnp.broadcast_to(c_ref[h, pl.ds(a, 1), :][:, cols], (pack, GATE_COLS)).astype(gdt)
                r2 = r2_ref[h, :, cols].reshape(P_NKEYS // pack, pack, GATE_COLS)
                e2 = e2_ref[h, :, cols].reshape(P_NKEYS // pack, pack, GATE_COLS)
                term = jnp.where(r2 < n_tile[None], e2 * c_tile[None], jnp.zeros((), gdt))
                gate = term if gate is None else gate + term
            g_act = _gelu(act[rows, cols].astype(gdt)).reshape(P_NKEYS // pack, pack, GATE_COLS)
            h_ref[rows, cols] = (g_act * gate).reshape(P_NKEYS, GATE_COLS)
    acc_ref[...] += _dot(vt_ref[...], h_ref[...])

    @pl.when(j == pl.num_programs(1) - 1)
    def _():
        y_ref[...] = acc_ref[...].T


def _peer_ffn(xt, u, vt, layer, route, tm, eb):
    d, n_tok = xt.shape
    tm = min(tm, n_tok)
    rspec = pl.BlockSpec((P_HEADS, P_NKEYS, tm), lambda i, j: (0, 0, i))
    return pl.pallas_call(
        _peer_ffn_kernel,
        grid=(n_tok // tm, P_NEXP // eb),
        in_specs=[pl.BlockSpec((d, tm), lambda i, j: (0, i)),
                  pl.BlockSpec((None, eb, d), lambda i, j: (layer, j, 0)),
                  pl.BlockSpec((None, d, eb), lambda i, j: (layer, 0, j)),
                  rspec, rspec, rspec, rspec],
        out_specs=pl.BlockSpec((tm, d), lambda i, j: (i, 0)),
        out_shape=jax.ShapeDtypeStruct((n_tok, d), jnp.float32),
        scratch_shapes=[pltpu.VMEM((d, tm), jnp.float32),
                        pltpu.VMEM((eb, tm), u.dtype)],
        compiler_params=_cparams("parallel", "arbitrary"),
        name="peer_ffn",
    )(xt, u, vt, *route)


def _chunk_tri():
    i = np.arange(LANES)
    same = (i[:, None] // HG_CHUNK) == (i[None, :] // HG_CHUNK)
    return jnp.asarray(same & (i[None, :] <= i[:, None]), jnp.bfloat16)


def kernel(x, w_in, hgrn_lb_logits, hgrn_norm_g, s5_lambda_re, s5_lambda_im, s5_log_step,
           s5_b_re, s5_b_im, s5_c_re, s5_c_im, s5_d, s5_w_glu, w_up_a, w_up_b, w_o,
           ln1_g, ln1_b, peer_w_q, peer_keys, peer_u, peer_v, ln2_g, ln2_b):
    bsz, n_tok, d = x.shape
    assert bsz == 1 and d == D_MODEL and n_tok % (S5_CHUNK * 8) == 0
    f32 = jnp.float32
    p = jax.nn.softmax(hgrn_lb_logits.astype(f32), axis=0)
    c = jnp.cumsum(p, axis=0)
    lbs = c - c[0:1]
    loglb = jnp.log(lbs)
    log1mlb = jnp.log1p(-lbs)
    tri = _chunk_tri()

    w_in_b, w_glu_b, w_ua_b, w_ub_b, w_o_b, w_q_b, u_b = (
        _bf(w) for w in (w_in, s5_w_glu, w_up_a, w_up_b, w_o, peer_w_q, peer_u))
    vt_b = _bf(jnp.swapaxes(peer_v, 1, 2))
    row = lambda p, l: p[l][None].astype(f32)

    xf = x.reshape(n_tok, d).astype(f32)
    xb = _bf(xf)
    for l in range(DEPTH):
        proj = _matmul(xb, w_in_b, l, f32, tm=MM_ROWS, tn=MM_COLS)
        oa = _hgrn2(proj, loglb[l:l + 1], log1mlb[l:l + 1], row(hgrn_norm_g, l), tri, tb=HG_ROWS)
        tables = _s5_tables(s5_lambda_re[l], s5_lambda_im[l], s5_log_step[l], s5_b_re[l],
                            s5_b_im[l], s5_c_re[l], s5_c_im[l], s5_d[l])
        yb = _s5(proj, tables, cb=S5_SCAN_CHUNKS, nrb=S5_ROW_BLOCKS)
        ob = _glu(yb, w_glu_b, l, tm=EPI_ROWS, tn=EPI_COLS)
        merged = _up_merge(oa, ob, w_ua_b, w_ub_b, l, proj, tm=EPI_ROWS, tn=EPI_COLS)
        xf, xb = _wo_ln(merged, w_o_b, l, xf, row(ln1_g, l), row(ln1_b, l), tm=LN_ROWS)
        q = _matmul(xb, w_q_b, l, f32, tm=MM_ROWS, tn=MM_COLS)
        route = _peer_route(q, peer_keys[l].astype(f32), tt=ROUTE_TOKENS)
        y = _peer_ffn(xb.T, u_b, vt_b, l, route, tm=PEER_TOKENS, eb=PEER_EXPERTS)
        xf, xb = _add_ln(xf, y, row(ln2_g, l), row(ln2_b, l), tm=LN_ROWS)
    return xf.reshape(bsz, n_tok, d).astype(x.dtype)
```

```python
import math

import jax
import jax.numpy as jnp
import numpy as np
from jax import lax
from jax.experimental import pallas as pl
from jax.experimental.pallas import tpu as pltpu

D_MODEL = 2048
DEPTH = 4
A_HEADS = 8
A_DIM = 128
A_WIDTH = A_HEADS * A_DIM
HG_CHUNK = 16
HG_UNROLL = 8
B_GROUPS = 64
B_GROUP_CH = 16
B_STATE = 64
B_WIDTH = B_GROUPS * B_GROUP_CH
S5_CHUNK = 16
S5_SLAB_GROUPS = 8
S5_EIG_CLIP = -1e-4
IN_COLS = 4 * A_WIDTH + B_WIDTH + 2 * D_MODEL
P_HEADS = 8
P_QDIM = 256
P_HALF = 128
P_NKEYS = 128
P_NEXP = P_NKEYS * P_NKEYS
P_TOPK = 16
ALPHA = (2.0 * DEPTH) ** 0.25
LN_EPS = 1e-5
RMS_EPS = 1e-6

LANES = 128
GATE_COLS = 128
VMEM_LIMIT = 52 * 1024 * 1024

MM_ROWS, MM_COLS = 1024, 1024
EPI_ROWS, EPI_COLS = 1024, 512
LN_ROWS = 512
HG_ROWS = 1024
S5_SCAN_CHUNKS = 64
S5_ROW_BLOCKS = 2
ROUTE_TOKENS = 128
PEER_TOKENS, PEER_EXPERTS = 512, 1024

_NT = (((1,), (1,)), ((), ()))
_TN = (((0,), (0,)), ((), ()))


def _cparams(*sem):
    return pltpu.CompilerParams(dimension_semantics=sem, vmem_limit_bytes=VMEM_LIMIT)


def _bf(x):
    return x.astype(jnp.bfloat16)


def _dot(a, b):
    return jnp.dot(a, b, preferred_element_type=jnp.float32)


def _sigmoid(x):
    return 1.0 / (1.0 + jnp.exp(-x))


def _gelu(x):
    c = math.sqrt(2.0 / math.pi)
    return 0.5 * x * (1.0 + jnp.tanh(c * (x + 0.044715 * (x * x * x))))


def _layer_norm(z, g, b):
    mu = jnp.mean(z, axis=-1, keepdims=True)
    zc = z - mu
    var = jnp.mean(zc * zc, axis=-1, keepdims=True)
    return zc * lax.rsqrt(var + LN_EPS) * g + b


def _matmul_kernel(a_ref, w_ref, o_ref):
    o_ref[...] = _dot(a_ref[...], w_ref[...]).astype(o_ref.dtype)


def _matmul(a, w, layer, out_dtype, tm, tn):
    m, k = a.shape
    n = w.shape[2]
    tm = min(tm, m)
    return pl.pallas_call(
        _matmul_kernel,
        grid=(n // tn, m // tm),
        in_specs=[pl.BlockSpec((tm, k), lambda j, i: (i, 0)),
                  pl.BlockSpec((None, k, tn), lambda j, i: (layer, 0, j))],
        out_specs=pl.BlockSpec((tm, tn), lambda j, i: (i, j)),
        out_shape=jax.ShapeDtypeStruct((m, n), out_dtype),
        compiler_params=_cparams("parallel", "parallel"),
        name="matmul",
    )(a, w)


def _hgrn2_kernel(q_ref, f_ref, i_ref, g_ref, loglb_ref, log1mlb_ref, ng_ref, tri_ref,
                  o_ref, st_ref, b_ref, k_ref):
    tb = q_ref.shape[0]

    @pl.when(pl.program_id(1) == 0)
    def _():
        st_ref[...] = jnp.zeros_like(st_ref)

    fz = f_ref[...]
    log_sig = jnp.minimum(fz, 0.0) - jnp.log1p(jnp.exp(-jnp.abs(fz)))
    z = log1mlb_ref[...] + log_sig
    a = loglb_ref[...]
    logf = jnp.maximum(a, z) + jnp.log1p(jnp.exp(-jnp.abs(a - z)))
    k_ref[...] = 1.0 - jnp.exp(logf)
    tri = tri_ref[...]
    for r in range(tb // LANES):
        rows = slice(r * LANES, (r + 1) * LANES)
        lf = logf[rows]
        hi = _bf(lf)
        r1 = lf - hi.astype(jnp.float32)
        mid = _bf(r1)
        lo = _bf(r1 - mid.astype(jnp.float32))
        b_ref[rows, :] = _dot(tri, hi) + _dot(tri, mid) + _dot(tri, lo)

    srow = lax.broadcasted_iota(jnp.int32, (HG_CHUNK, 1), 0)
    ng = ng_ref[...]

    def chunk(j, st):
        rows = pl.ds(pl.multiple_of(j * HG_CHUNK, HG_CHUNK), HG_CHUNK)
        b = b_ref[rows, :]
        q = q_ref[rows, :]
        k = k_ref[rows, :]
        v = i_ref[rows, :]
        b_last = b[HG_CHUNK - 1:HG_CHUNK, :]
        o = lax.dot_general(_bf(q * jnp.exp(b)), _bf(st), _NT,
                            preferred_element_type=jnp.float32)
        intra = []
        for t in range(HG_CHUNK):
            diff = jnp.where(srow <= t, b[t:t + 1, :] - b, -jnp.inf)
            p = (q[t:t + 1, :] * k) * jnp.exp(diff)
            att = jnp.sum(p, axis=-1, keepdims=True)
            intra.append(jnp.sum(att * v, axis=0, keepdims=True))
        o = o + jnp.concatenate(intra, axis=0)
        k_tail = k * jnp.exp(b_last - b)
        st = st * jnp.exp(b_last) + lax.dot_general(
            _bf(v), _bf(k_tail), _TN, preferred_element_type=jnp.float32)
        o = o * lax.rsqrt(jnp.mean(o * o, axis=-1, keepdims=True) + RMS_EPS)
        o = o * ng * _sigmoid(g_ref[rows, :])
        o_ref[rows, :] = o.astype(o_ref.dtype)
        return st

    st_ref[...] = lax.fori_loop(0, tb // HG_CHUNK, chunk, st_ref[...], unroll=HG_UNROLL)


def _hgrn2(proj, loglb, log1mlb, norm_g, tri, tb):
    n_tok = proj.shape[0]
    tb = min(tb, n_tok)

    def col(off):
        return pl.BlockSpec((tb, A_DIM), lambda h, i: (i, off + h))

    def par():
        return pl.BlockSpec((1, A_DIM), lambda h, i: (0, h))

    return pl.pallas_call(
        _hgrn2_kernel,
        grid=(A_HEADS, n_tok // tb),
        in_specs=[col(0), col(A_HEADS), col(2 * A_HEADS), col(3 * A_HEADS),
                  par(), par(), par(),
                  pl.BlockSpec((LANES, LANES), lambda h, i: (0, 0))],
        out_specs=pl.BlockSpec((tb, A_DIM), lambda h, i: (i, h)),
        out_shape=jax.ShapeDtypeStruct((n_tok, A_WIDTH), jnp.bfloat16),
        scratch_shapes=[pltpu.VMEM((A_DIM, A_DIM), jnp.float32),
                        pltpu.VMEM((tb, A_DIM), jnp.float32),
                        pltpu.VMEM((tb, A_DIM), jnp.float32)],
        compiler_params=_cparams("parallel", "arbitrary"),
        name="hgrn2",
    )(proj, proj, proj, proj, loglb, log1mlb, norm_g, tri)


def _s5_tables(lam_re, lam_im, log_step, b_re, b_im, c_re, c_im, d):
    f32 = jnp.float32
    hp = lax.Precision.HIGHEST
    t = S5_CHUNK
    lr = jnp.minimum(lam_re.astype(f32), S5_EIG_CLIP)
    li = lam_im.astype(f32)
    dt = jnp.exp(log_step.astype(f32))[:, None]
    mag = jnp.exp(lr * dt)
    ar = mag * jnp.cos(li * dt)
    ai = mag * jnp.sin(li * dt)
    den = lr * lr + li * li
    nr = ar - 1.0
    zr = (nr * lr + ai * li) / den
    zi = (ai * lr - nr * li) / den
    br_, bi_ = b_re.astype(f32), b_im.astype(f32)
    bbr = zr[..., None] * br_ - zi[..., None] * bi_
    bbi = zr[..., None] * bi_ + zi[..., None] * br_
    pr = [jnp.ones_like(ar)]
    pi = [jnp.zeros_like(ai)]
    for _ in range(t):
        r0, i0 = pr[-1], pi[-1]
        pr.append(r0 * ar - i0 * ai)
        pi.append(r0 * ai + i0 * ar)
    pwr = jnp.stack(pr, axis=1)
    pwi = jnp.stack(pi, axis=1)
    cr, ci = c_re.astype(f32), c_im.astype(f32)
    car = cr[:, None] * pwr[:, :t, None, :] - ci[:, None] * pwi[:, :t, None, :]
    cai = cr[:, None] * pwi[:, :t, None, :] + ci[:, None] * pwr[:, :t, None, :]
    kk = (jnp.einsum('gtnp,gpm->gtnm', car, bbr, precision=hp)
          - jnp.einsum('gtnp,gpm->gtnm', cai, bbi, precision=hp))
    ti = np.arange(t)
    lag = ti[None, :] - ti[:, None]
    toep = kk[:, np.clip(lag, 0, t - 1)]
    toep = jnp.where((lag >= 0)[None, :, :, None, None], toep, 0.0)
    sg = S5_SLAB_GROUPS
    ns = B_GROUPS // sg
    ch = B_GROUP_CH
    ns2 = 2 * B_STATE
    toep = jnp.transpose(toep.reshape(ns, sg, t, t, ch, ch), (0, 2, 1, 5, 3, 4))
    toep = _bf(toep.reshape(ns, t * LANES, t * ch))
    er = pwr[:, t - 1 - ti][:, :, None, :]
    ei = pwi[:, t - 1 - ti][:, :, None, :]
    bbr_t = jnp.transpose(bbr, (0, 2, 1))[:, None]
    bbi_t = jnp.transpose(bbi, (0, 2, 1))[:, None]
    w1 = jnp.concatenate([er * bbr_t - ei * bbi_t, er * bbi_t + ei * bbr_t], axis=-1)
    w1 = jnp.transpose(w1.reshape(ns, sg, t, ch, ns2), (0, 2, 1, 3, 4))
    w1 = _bf(w1.reshape(ns, t * LANES, ns2))
    zr1 = pwr[:, 1:, None, :]
    zi1 = pwi[:, 1:, None, :]
    w2r = cr[:, None] * zr1 - ci[:, None] * zi1
    w2i = -(cr[:, None] * zi1 + ci[:, None] * zr1)
    w2 = jnp.concatenate([w2r, w2i], axis=-1)
    w2 = jnp.transpose(w2.reshape(ns, sg, t, ch, ns2), (0, 1, 4, 2, 3))
    w2 = _bf(w2.reshape(ns, sg * ns2, t * ch))
    a1 = jnp.concatenate([pwr[:, t], pwr[:, t]], axis=-1)
    a2 = jnp.concatenate([-pwi[:, t], pwi[:, t]], axis=-1)
    dd = jnp.tile(d.astype(f32).reshape(ns, 1, LANES), (1, 1, t))
    return toep, w1, w2, a1, a2, dd


def _s5_spread_consts():
    t, sg, ch, ns2 = S5_CHUNK, S5_SLAB_GROUPS, B_GROUP_CH, 2 * B_STATE
    grp = np.arange(LANES) // ch
    col_un = np.kron(np.eye(t), np.kron(np.ones((1, sg)), np.eye(ch)))
    col_p = np.kron(np.ones((1, sg)), np.eye(ns2))
    keep_un = grp[:, None] == np.tile(grp, t)[None, :]
    keep_p = grp[:, None] == np.repeat(np.arange(sg), ns2)[None, :]
    keep_g = np.arange(sg)[:, None] == np.tile(grp, t)[None, :]
    bf = lambda m: jnp.asarray(m, jnp.bfloat16)
    f = lambda m: jnp.asarray(m, jnp.float32)
    return bf(col_un), bf(col_p), f(keep_un), f(keep_p), f(keep_g)


def _s5_rows(u_ref, nc):
    return jnp.concatenate([u_ref[pl.ds(t, nc, stride=S5_CHUNK), :] for t in range(S5_CHUNK)],
                           axis=1)


def _s5_local_kernel(u_ref, w1c_ref, colp_ref, keepp_ref, e_ref, w1_ref):
    @pl.when(pl.program_id(1) == 0)
    def _():
        for t in range(S5_CHUNK):
            rows = slice(t * LANES, (t + 1) * LANES)
            w1_ref[rows, :] = _bf(_dot(w1c_ref[0, rows, :], colp_ref[...]) * keepp_ref[...])

    e_ref[...] = _dot(_bf(_s5_rows(u_ref, e_ref.shape[0])), w1_ref[...])


def _s5_scan_kernel(e_ref, a1_ref, a2_ref, xp_ref, st_ref):
    @pl.when(pl.program_id(0) == 0)
    def _():
        st_ref[...] = jnp.zeros_like(st_ref)

    a1 = a1_ref[...]
    a2 = a2_ref[...]

    def step(c, carry):
        x, xs = carry
        xp_ref[c] = x
        e = e_ref[c]
        return a1 * x + a2 * xs + e, a1 * xs - a2 * x + pltpu.roll(e, B_STATE, 1)

    x0 = st_ref[...]
    x, _ = lax.fori_loop(0, e_ref.shape[0], step, (x0, pltpu.roll(x0, B_STATE, 1)), unroll=8)
    st_ref[...] = x


def _s5_out_kernel(u_ref, xp_ref, tc_ref, w2c_ref, colun_ref, keepun_ref, keepg_ref, d_ref,
                   y_ref, toep_ref, w2_ref):
    nc = xp_ref.shape[0]

    @pl.when(pl.program_id(1) == 0)
    def _():
        for t in range(S5_CHUNK):
            rows = slice(t * LANES, (t + 1) * LANES)
            toep_ref[rows, :] = _bf(_dot(tc_ref[0, rows, :], colun_ref[...]) * keepun_ref[...])
        for g in range(S5_SLAB_GROUPS):
            rows = slice(g * 2 * B_STATE, (g + 1) * 2 * B_STATE)
            w2_ref[rows, :] = _bf(_dot(w2c_ref[0, rows, :], colun_ref[...]) * keepg_ref[g:g + 1, :])

    x = _s5_rows(u_ref, nc)
    y = _dot(_bf(x), toep_ref[...]) + _dot(_bf(xp_ref[...]), w2_ref[...]) + d_ref[0] * x
    y = _gelu(y)
    for t in range(S5_CHUNK):
        y_ref[pl.ds(t, nc, stride=S5_CHUNK), :] = y[:, t * LANES:(t + 1) * LANES]


def _s5(proj, tables, cb, nrb):
    toep, w1, w2, a1, a2, dd = tables
    col_un, col_p, keep_un, keep_p, keep_g = _s5_spread_consts()
    n_tok = proj.shape[0]
    nc = n_tok // S5_CHUNK
    g = B_GROUPS
    ns = 2 * B_STATE
    n_slab = B_WIDTH // LANES
    sw = S5_SLAB_GROUPS * ns
    rw = S5_CHUNK * LANES
    u0 = 4 * A_WIDTH // LANES
    cb = min(cb, nc)
    ncb = nc // nrb
    tokb = n_tok // nrb
    fixed = lambda shape: pl.BlockSpec(shape, lambda s, r: (0,) * len(shape))
    slab = lambda shape: pl.BlockSpec((1,) + shape, lambda s, r: (s, 0, 0))
    e = pl.pallas_call(
        _s5_local_kernel,
        grid=(n_slab, nrb),
        in_specs=[pl.BlockSpec((tokb, LANES), lambda s, r: (r, u0 + s)),
                  slab(w1.shape[1:]), fixed(col_p.shape), fixed(keep_p.shape)],
        out_specs=pl.BlockSpec((ncb, sw), lambda s, r: (r, s)),
        out_shape=jax.ShapeDtypeStruct((nc, g * ns), jnp.float32),
        scratch_shapes=[pltpu.VMEM((rw, sw), jnp.bfloat16)],
        compiler_params=_cparams("parallel", "arbitrary"),
        name="s5_local",
    )(proj, w1, col_p, keep_p)
    xp = pl.pallas_call(
        _s5_scan_kernel,
        grid=(nc // cb,),
        in_specs=[pl.BlockSpec((cb, g, ns), lambda i: (i, 0, 0)),
                  pl.BlockSpec((g, ns), lambda i: (0, 0)),
                  pl.BlockSpec((g, ns), lambda i: (0, 0))],
        out_specs=pl.BlockSpec((cb, g, ns), lambda i: (i, 0, 0)),
        out_shape=jax.ShapeDtypeStruct((nc, g, ns), jnp.float32),
        scratch_shapes=[pltpu.VMEM((g, ns), jnp.float32)],
        compiler_params=_cparams("arbitrary"),
        name="s5_scan",
    )(e.reshape(nc, g, ns), a1, a2)
    return pl.pallas_call(
        _s5_out_kernel,
        grid=(n_slab, nrb),
        in_specs=[pl.BlockSpec((tokb, LANES), lambda s, r: (r, u0 + s)),
                  pl.BlockSpec((ncb, sw), lambda s, r: (r, s)),
                  slab(toep.shape[1:]), slab(w2.shape[1:]),
                  fixed(col_un.shape), fixed(keep_un.shape), fixed(keep_g.shape),
                  slab((1, rw))],
        out_specs=pl.BlockSpec((tokb, LANES), lambda s, r: (r, s)),
        out_shape=jax.ShapeDtypeStruct((n_tok, B_WIDTH), jnp.float32),
        scratch_shapes=[pltpu.VMEM((rw, rw), jnp.bfloat16),
                        pltpu.VMEM((sw, rw), jnp.bfloat16)],
        compiler_params=_cparams("parallel", "arbitrary"),
        name="s5_out",
    )(proj, xp.reshape(nc, g * ns), toep, w2, col_un, keep_un, keep_g, dd)


def _glu_kernel(y_ref, wa_ref, wb_ref, o_ref):
    y = _bf(y_ref[...])
    o_ref[...] = (_dot(y, wa_ref[...]) * _sigmoid(_dot(y, wb_ref[...]))).astype(o_ref.dtype)


def _glu(y, w, layer, tm, tn):
    m, k = y.shape
    n = w.shape[2] // 2
    tm = min(tm, m)
    nb = n // tn
    return pl.pallas_call(
        _glu_kernel,
        grid=(nb, m // tm),
        in_specs=[pl.BlockSpec((tm, k), lambda j, i: (i, 0)),
                  pl.BlockSpec((None, k, tn), lambda j, i: (layer, 0, j)),
                  pl.BlockSpec((None, k, tn), lambda j, i: (layer, 0, nb + j))],
        out_specs=pl.BlockSpec((tm, tn), lambda j, i: (i, j)),
        out_shape=jax.ShapeDtypeStruct((m, n), jnp.bfloat16),
        compiler_params=_cparams("parallel", "parallel"),
        name="glu",
    )(y, w, w)


def _up_merge_kernel(oa_ref, ob_ref, wa_ref, wb_ref, ga_ref, gb_ref, o_ref):
    m = (_sigmoid(ga_ref[...]) * _dot(oa_ref[...], wa_ref[...])
         + _sigmoid(gb_ref[...]) * _dot(ob_ref[...], wb_ref[...]))
    o_ref[...] = m.astype(o_ref.dtype)


def _up_merge(oa, ob, wa, wb, layer, proj, tm, tn):
    m, k = oa.shape
    n = wa.shape[2]
    tm = min(tm, m)
    ga0 = (4 * A_WIDTH + B_WIDTH) // tn
    gb0 = ga0 + D_MODEL // tn
    return pl.pallas_call(
        _up_merge_kernel,
        grid=(n // tn, m // tm),
        in_specs=[pl.BlockSpec((tm, k), lambda j, i: (i, 0)),
                  pl.BlockSpec((tm, k), lambda j, i: (i, 0)),
                  pl.BlockSpec((None, k, tn), lambda j, i: (layer, 0, j)),
                  pl.BlockSpec((None, k, tn), lambda j, i: (layer, 0, j)),
                  pl.BlockSpec((tm, tn), lambda j, i: (i, ga0 + j)),
                  pl.BlockSpec((tm, tn), lambda j, i: (i, gb0 + j))],
        out_specs=pl.BlockSpec((tm, tn), lambda j, i: (i, j)),
        out_shape=jax.ShapeDtypeStruct((m, n), jnp.bfloat16),
        compiler_params=_cparams("parallel", "parallel"),
        name="up_merge",
    )(oa, ob, wa, wb, proj, proj)


def _wo_ln_kernel(m_ref, w_ref, x_ref, g_ref, b_ref, o_ref, ob_ref):
    z = ALPHA * x_ref[...] + _dot(m_ref[...], w_ref[...])
    y = _layer_norm(z, g_ref[...], b_ref[...])
    o_ref[...] = y
    ob_ref[...] = _bf(y)


def _wo_ln(merged, w, layer, x, g, b, tm):
    m, k = merged.shape
    n = w.shape[2]
    tm = min(tm, m)
    row = lambda i: (i, 0)
    fix = lambda i: (0, 0)
    return pl.pallas_call(
        _wo_ln_kernel,
        grid=(m // tm,),
        in_specs=[pl.BlockSpec((tm, k), row), pl.BlockSpec((None, k, n), lambda i: (layer, 0, 0)),
                  pl.BlockSpec((tm, n), row), pl.BlockSpec((1, n), fix),
                  pl.BlockSpec((1, n), fix)],
        out_specs=[pl.BlockSpec((tm, n), row), pl.BlockSpec((tm, n), row)],
        out_shape=[jax.ShapeDtypeStruct((m, n), jnp.float32),
                   jax.ShapeDtypeStruct((m, n), jnp.bfloat16)],
        compiler_params=_cparams("parallel"),
        name="wo_ln",
    )(merged, w, x, g, b)


def _add_ln_kernel(x_ref, y_ref, g_ref, b_ref, o_ref, ob_ref):
    y = _layer_norm(ALPHA * x_ref[...] + y_ref[...], g_ref[...], b_ref[...])
    o_ref[...] = y
    ob_ref[...] = _bf(y)


def _add_ln(x, y, g, b, tm):
    m, n = x.shape
    tm = min(tm, m)
    row = lambda i: (i, 0)
    fix = lambda i: (0, 0)
    return pl.pallas_call(
        _add_ln_kernel,
        grid=(m // tm,),
        in_specs=[pl.BlockSpec((tm, n), row), pl.BlockSpec((tm, n), row),
                  pl.BlockSpec((1, n), fix), pl.BlockSpec((1, n), fix)],
        out_specs=[pl.BlockSpec((tm, n), row), pl.BlockSpec((tm, n), row)],
        out_shape=[jax.ShapeDtypeStruct((m, n), jnp.float32),
                   jax.ShapeDtypeStruct((m, n), jnp.bfloat16)],
        compiler_params=_cparams("parallel"),
        name="add_ln",
    )(x, y, g, b)


def _take_top(s, n_take, on_take):
    rows = s.shape[0]
    rid = lax.broadcasted_iota(jnp.int32, s.shape, 0)
    for k in range(n_take):
        m = jnp.max(s, axis=0, keepdims=True)
        idx = jnp.min(jnp.where(s == m, rid, rows), axis=0, keepdims=True)
        on_take(k, m, idx)
        s = jnp.where(rid == idx, -jnp.inf, s)


def _peer_route_kernel(q_ref, keys_ref, r2_ref, e2_ref, n_ref, c_ref):
    tt = q_ref.shape[0]
    kid = lax.broadcasted_iota(jnp.int32, (P_NKEYS, tt), 0)
    rank_id = lax.broadcasted_iota(jnp.int32, (P_TOPK, tt), 0)
    for h in range(P_HEADS):
        s = []
        for half in range(2):
            qh = q_ref[:, (2 * h + half) * P_HALF:(2 * h + half + 1) * P_HALF]
            s.append(lax.dot_general(_bf(keys_ref[h, half]), _bf(qh), _NT,
                                     preferred_element_type=jnp.float32))
        vals = [[], []]
        idxs = [[], []]
        rank2 = [jnp.full((P_NKEYS, tt), P_TOPK, jnp.int32)]
        for half in range(2):
            def take(k, m, idx, half=half):
                vals[half].append(m)
                idxs[half].append(idx)
                if half == 1:
                    rank2[0] = jnp.where(kid == idx, k, rank2[0])
            _take_top(s[half], P_TOPK, take)
        v1 = jnp.concatenate(vals[0], axis=0)
        v2 = jnp.concatenate(vals[1], axis=0)
        widths = [P_TOPK // (i + 1) for i in range(P_TOPK)]
        starts = np.cumsum([0] + widths)
        pad = int(-starts[-1] % 8)
        cand = jnp.concatenate([v1[i:i + 1] + v2[:widths[i]] for i in range(P_TOPK)]
                               + [jnp.full((pad, tt), -jnp.inf, jnp.float32)], axis=0)
        state = [jnp.zeros((P_TOPK, tt), jnp.int32), jnp.zeros((1, tt), jnp.float32)]
        top = v1[0:1] + v2[0:1]

        def take_c(k, m, pos):
            rank1 = sum((pos >= int(st)).astype(jnp.int32) for st in starts[1:P_TOPK])
            state[0] = state[0] + (rank_id == rank1).astype(jnp.int32)
            state[1] = state[1] + jnp.exp(m - top)
        _take_top(cand, P_TOPK, take_c)
        n_rank, z = state
        n_key = jnp.zeros((P_NKEYS, tt), jnp.int32)
        for i in range(P_TOPK):
            n_key = jnp.where(kid == idxs[0][i], n_rank[i:i + 1], n_key)
        r2_ref[h] = rank2[0].astype(jnp.float32).astype(r2_ref.dtype)
        n_ref[h] = n_key.astype(jnp.float32)
        e2_ref[h] = jnp.exp(s[1] - v2[0:1]).astype(e2_ref.dtype)
        c_ref[h] = jnp.exp(s[0] - v1[0:1]) / z


def _peer_route(q, keys, tt, gate_dtype=jnp.bfloat16):
    n_tok = q.shape[0]
    tt = min(tt, n_tok)
    out = [jax.ShapeDtypeStruct((P_HEADS, P_NKEYS, n_tok), dt)
           for dt in (gate_dtype, gate_dtype, jnp.float32, jnp.float32)]
    ospec = pl.BlockSpec((P_HEADS, P_NKEYS, tt), lambda i: (0, 0, i))
    return pl.pallas_call(
        _peer_route_kernel,
        grid=(n_tok // tt,),
        in_specs=[pl.BlockSpec((tt, P_HEADS * P_QDIM), lambda i: (i, 0)),
                  pl.BlockSpec((P_HEADS, 2, P_NKEYS, P_HALF), lambda i: (0, 0, 0, 0))],
        out_specs=[ospec] * 4,
        out_shape=out,
        compiler_params=_cparams("parallel"),
        name="peer_route",
    )(q, keys)


def _peer_ffn_kernel(xt_ref, u_ref, vt_ref, r2_ref, e2_ref, n_ref, c_ref, y_ref, acc_ref, h_ref):
    j = pl.program_id(1)
    eb = u_ref.shape[0]
    tm = xt_ref.shape[1]

    @pl.when(j == 0)
    def _():
        acc_ref[...] = jnp.zeros_like(acc_ref)

    act = _dot(u_ref[...], xt_ref[...])
    gdt = h_ref.dtype
    pack = 8 * 4 // jnp.dtype(gdt).itemsize
    for al in range(eb // P_NKEYS):
        a = j * (eb // P_NKEYS) + al
        rows = slice(al * P_NKEYS, (al + 1) * P_NKEYS)
        for cc in range(tm // GATE_COLS):
            cols = slice(cc * GATE_COLS, (cc + 1) * GATE_COLS)
            gate = None
            for h in range(P_HEADS):
                n_tile = jnp.broadcast_to(n_ref[h, pl.ds(a, 1), :][:, cols], (pack, GATE_COLS)).astype(gdt)
                c_tile = jnp.broadcast_to(c_ref[h, pl.ds(a, 1), :][:, cols], (pack, GATE_COLS)).astype(gdt)
                r2 = r2_ref[h, :, cols].reshape(P_NKEYS // pack, pack, GATE_COLS)
                e2 = e2_ref[h, :, cols].reshape(P_NKEYS // pack, pack, GATE_COLS)
                term = jnp.where(r2 < n_tile[None], e2 * c_tile[None], jnp.zeros((), gdt))
                gate = term if gate is None else gate + term
            g_act = _gelu(act[rows, cols].astype(gdt)).reshape(P_NKEYS // pack, pack, GATE_COLS)
            h_ref[rows, cols] = (g_act * gate).reshape(P_NKEYS, GATE_COLS)
    acc_ref[...] += _dot(vt_ref[...], h_ref[...])

    @pl.when(j == pl.num_programs(1) - 1)
    def _():
        y_ref[...] = acc_ref[...].T


def _peer_ffn(xt, u, vt, layer, route, tm, eb):
    d, n_tok = xt.shape
    tm = min(tm, n_tok)
    rspec = pl.BlockSpec((P_HEADS, P_NKEYS, tm), lambda i, j: (0, 0, i))
    return pl.pallas_call(
        _peer_ffn_kernel,
        grid=(n_tok // tm, P_NEXP // eb),
        in_specs=[pl.BlockSpec((d, tm), lambda i, j: (0, i)),
                  pl.BlockSpec((None, eb, d), lambda i, j: (layer, j, 0)),
                  pl.BlockSpec((None, d, eb), lambda i, j: (layer, 0, j)),
                  rspec, rspec, rspec, rspec],
        out_specs=pl.BlockSpec((tm, d), lambda i, j: (i, 0)),
        out_shape=jax.ShapeDtypeStruct((n_tok, d), jnp.float32),
        scratch_shapes=[pltpu.VMEM((d, tm), jnp.float32),
                        pltpu.VMEM((eb, tm), u.dtype)],
        compiler_params=_cparams("parallel", "arbitrary"),
        name="peer_ffn",
    )(xt, u, vt, *route)


def _chunk_tri():
    i = np.arange(LANES)
    same = (i[:, None] // HG_CHUNK) == (i[None, :] // HG_CHUNK)
    return jnp.asarray(same & (i[None, :] <= i[:, None]), jnp.bfloat16)


def kernel(x, w_in, hgrn_lb_logits, hgrn_norm_g, s5_lambda_re, s5_lambda_im, s5_log_step,
           s5_b_re, s5_b_im, s5_c_re, s5_c_im, s5_d, s5_w_glu, w_up_a, w_up_b, w_o,
           ln1_g, ln1_b, peer_w_q, peer_keys, peer_u, peer_v, ln2_g, ln2_b):
    bsz, n_tok, d = x.shape
    assert bsz == 1 and d == D_MODEL and n_tok % (S5_CHUNK * 8) == 0
    f32 = jnp.float32
    p = jax.nn.softmax(hgrn_lb_logits.astype(f32), axis=0)
    c = jnp.cumsum(p, axis=0)
    lbs = c - c[0:1]
    loglb = jnp.log(lbs)
    log1mlb = jnp.log1p(-lbs)
    tri = _chunk_tri()

    w_in_b, w_glu_b, w_ua_b, w_ub_b, w_o_b, w_q_b, u_b = (
        _bf(w) for w in (w_in, s5_w_glu, w_up_a, w_up_b, w_o, peer_w_q, peer_u))
    vt_b = _bf(jnp.swapaxes(peer_v, 1, 2))
    row = lambda p, l: p[l][None].astype(f32)

    xf = x.reshape(n_tok, d).astype(f32)
    xb = _bf(xf)
    for l in range(DEPTH):
        proj = _matmul(xb, w_in_b, l, f32, tm=MM_ROWS, tn=MM_COLS)
        oa = _hgrn2(proj, loglb[l:l + 1], log1mlb[l:l + 1], row(hgrn_norm_g, l), tri, tb=HG_ROWS)
        tables = _s5_tables(s5_lambda_re[l], s5_lambda_im[l], s5_log_step[l], s5_b_re[l],
                            s5_b_im[l], s5_c_re[l], s5_c_im[l], s5_d[l])
        yb = _s5(proj, tables, cb=S5_SCAN_CHUNKS, nrb=S5_ROW_BLOCKS)
        ob = _glu(yb, w_glu_b, l, tm=EPI_ROWS, tn=EPI_COLS)
        merged = _up_merge(oa, ob, w_ua_b, w_ub_b, l, proj, tm=EPI_ROWS, tn=EPI_COLS)
        xf, xb = _wo_ln(merged, w_o_b, l, xf, row(ln1_g, l), row(ln1_b, l), tm=LN_ROWS)
        q = _matmul(xb, w_q_b, l, f32, tm=MM_ROWS, tn=MM_COLS)
        route = _peer_route(q, peer_keys[l].astype(f32), tt=ROUTE_TOKENS)
        y = _peer_ffn(xb.T, u_b, vt_b, l, route, tm=PEER_TOKENS, eb=PEER_EXPERTS)
        xf, xb = _add_ln(xf, y, row(ln2_g, l), row(ln2_b, l), tm=LN_ROWS)
    return xf.reshape(bsz, n_tok, d).astype(x.dtype)
```

```python
import math

import jax
import jax.numpy as jnp
import numpy as np
from jax import lax
from jax.experimental import pallas as pl
from jax.experimental.pallas import tpu as pltpu

D_MODEL = 2048
DEPTH = 4
A_HEADS = 8
A_DIM = 128
A_WIDTH = A_HEADS * A_DIM
HG_CHUNK = 16
HG_UNROLL = 8
B_GROUPS = 64
B_GROUP_CH = 16
B_STATE = 64
B_WIDTH = B_GROUPS * B_GROUP_CH
S5_CHUNK = 16
S5_SLAB_GROUPS = 8
S5_EIG_CLIP = -1e-4
IN_COLS = 4 * A_WIDTH + B_WIDTH + 2 * D_MODEL
P_HEADS = 8
P_QDIM = 256
P_HALF = 128
P_NKEYS = 128
P_NEXP = P_NKEYS * P_NKEYS
P_TOPK = 16
ALPHA = (2.0 * DEPTH) ** 0.25
LN_EPS = 1e-5
RMS_EPS = 1e-6

LANES = 128
GATE_COLS = 128
VMEM_LIMIT = 52 * 1024 * 1024

MM_ROWS, MM_COLS = 1024, 1024
EPI_ROWS, EPI_COLS = 1024, 512
LN_ROWS = 512
HG_ROWS = 1024
S5_SCAN_CHUNKS = 64
S5_ROW_BLOCKS = 2
ROUTE_TOKENS = 128
PEER_TOKENS, PEER_EXPERTS = 512, 1024

_NT = (((1,), (1,)), ((), ()))
_TN = (((0,), (0,)), ((), ()))


def _cparams(*sem):
    return pltpu.CompilerParams(dimension_semantics=sem, vmem_limit_bytes=VMEM_LIMIT)


def _bf(x):
    return x.astype(jnp.bfloat16)


def _dot(a, b):
    return jnp.dot(a, b, preferred_element_type=jnp.float32)


def _sigmoid(x):
    return 1.0 / (1.0 + jnp.exp(-x))


def _gelu(x):
    c = math.sqrt(2.0 / math.pi)
    return 0.5 * x * (1.0 + jnp.tanh(c * (x + 0.044715 * (x * x * x))))


def _layer_norm(z, g, b):
    mu = jnp.mean(z, axis=-1, keepdims=True)
    zc = z - mu
    var = jnp.mean(zc * zc, axis=-1, keepdims=True)
    return zc * lax.rsqrt(var + LN_EPS) * g + b


def _matmul_kernel(a_ref, w_ref, o_ref):
    o_ref[...] = _dot(a_ref[...], w_ref[...]).astype(o_ref.dtype)


def _matmul(a, w, layer, out_dtype, tm, tn):
    m, k = a.shape
    n = w.shape[2]
    tm = min(tm, m)
    return pl.pallas_call(
        _matmul_kernel,
        grid=(n // tn, m // tm),
        in_specs=[pl.BlockSpec((tm, k), lambda j, i: (i, 0)),
                  pl.BlockSpec((None, k, tn), lambda j, i: (layer, 0, j))],
        out_specs=pl.BlockSpec((tm, tn), lambda j, i: (i, j)),
        out_shape=jax.ShapeDtypeStruct((m, n), out_dtype),
        compiler_params=_cparams("parallel", "parallel"),
        name="matmul",
    )(a, w)


def _hgrn2_kernel(q_ref, f_ref, i_ref, g_ref, loglb_ref, log1mlb_ref, ng_ref, tri_ref,
                  o_ref, st_ref, b_ref, k_ref):
    tb = q_ref.shape[0]

    @pl.when(pl.program_id(1) == 0)
    def _():
        st_ref[...] = jnp.zeros_like(st_ref)

    fz = f_ref[...]
    log_sig = jnp.minimum(fz, 0.0) - jnp.log1p(jnp.exp(-jnp.abs(fz)))
    z = log1mlb_ref[...] + log_sig
    a = loglb_ref[...]
    logf = jnp.maximum(a, z) + jnp.log1p(jnp.exp(-jnp.abs(a - z)))
    k_ref[...] = 1.0 - jnp.exp(logf)
    tri = tri_ref[...]
    for r in range(tb // LANES):
        rows = slice(r * LANES, (r + 1) * LANES)
        lf = logf[rows]
        hi = _bf(lf)
        r1 = lf - hi.astype(jnp.float32)
        mid = _bf(r1)
        lo = _bf(r1 - mid.astype(jnp.float32))
        b_ref[rows, :] = _dot(tri, hi) + _dot(tri, mid) + _dot(tri, lo)

    srow = lax.broadcasted_iota(jnp.int32, (HG_CHUNK, 1), 0)
    ng = ng_ref[...]

    def chunk(j, st):
        rows = pl.ds(pl.multiple_of(j * HG_CHUNK, HG_CHUNK), HG_CHUNK)
        b = b_ref[rows, :]
        q = q_ref[rows, :]
        k = k_ref[rows, :]
        v = i_ref[rows, :]
        b_last = b[HG_CHUNK - 1:HG_CHUNK, :]
        o = lax.dot_general(_bf(q * jnp.exp(b)), _bf(st), _NT,
                            preferred_element_type=jnp.float32)
        intra = []
        for t in range(HG_CHUNK):
            diff = jnp.where(srow <= t, b[t:t + 1, :] - b, -jnp.inf)
            p = (q[t:t + 1, :] * k) * jnp.exp(diff)
            att = jnp.sum(p, axis=-1, keepdims=True)
            intra.append(jnp.sum(att * v, axis=0, keepdims=True))
        o = o + jnp.concatenate(intra, axis=0)
        k_tail = k * jnp.exp(b_last - b)
        st = st * jnp.exp(b_last) + lax.dot_general(
            _bf(v), _bf(k_tail), _TN, preferred_element_type=jnp.float32)
        o = o * lax.rsqrt(jnp.mean(o * o, axis=-1, keepdims=True) + RMS_EPS)
        o = o * ng * _sigmoid(g_ref[rows, :])
        o_ref[rows, :] = o.astype(o_ref.dtype)
        return st

    st_ref[...] = lax.fori_loop(0, tb // HG_CHUNK, chunk, st_ref[...], unroll=HG_UNROLL)


def _hgrn2(proj, loglb, log1mlb, norm_g, tri, tb):
    n_tok = proj.shape[0]
    tb = min(tb, n_tok)

    def col(off):
        return pl.BlockSpec((tb, A_DIM), lambda h, i: (i, off + h))

    def par():
        return pl.BlockSpec((1, A_DIM), lambda h, i: (0, h))

    return pl.pallas_call(
        _hgrn2_kernel,
        grid=(A_HEADS, n_tok // tb),
        in_specs=[col(0), col(A_HEADS), col(2 * A_HEADS), col(3 * A_HEADS),
                  par(), par(), par(),
                  pl.BlockSpec((LANES, LANES), lambda h, i: (0, 0))],
        out_specs=pl.BlockSpec((tb, A_DIM), lambda h, i: (i, h)),
        out_shape=jax.ShapeDtypeStruct((n_tok, A_WIDTH), jnp.bfloat16),
        scratch_shapes=[pltpu.VMEM((A_DIM, A_DIM), jnp.float32),
                        pltpu.VMEM((tb, A_DIM), jnp.float32),
                        pltpu.VMEM((tb, A_DIM), jnp.float32)],
        compiler_params=_cparams("parallel", "arbitrary"),
        name="hgrn2",
    )(proj, proj, proj, proj, loglb, log1mlb, norm_g, tri)


def _s5_tables(lam_re, lam_im, log_step, b_re, b_im, c_re, c_im, d):
    f32 = jnp.float32
    hp = lax.Precision.HIGHEST
    t = S5_CHUNK
    lr = jnp.minimum(lam_re.astype(f32), S5_EIG_CLIP)
    li = lam_im.astype(f32)
    dt = jnp.exp(log_step.astype(f32))[:, None]
    mag = jnp.exp(lr * dt)
    ar = mag * jnp.cos(li * dt)
    ai = mag * jnp.sin(li * dt)
    den = lr * lr + li * li
    nr = ar - 1.0
    zr = (nr * lr + ai * li) / den
    zi = (ai * lr - nr * li) / den
    br_, bi_ = b_re.astype(f32), b_im.astype(f32)
    bbr = zr[..., None] * br_ - zi[..., None] * bi_
    bbi = zr[..., None] * bi_ + zi[..., None] * br_
    pr = [jnp.ones_like(ar)]
    pi = [jnp.zeros_like(ai)]
    for _ in range(t):
        r0, i0 = pr[-1], pi[-1]
        pr.append(r0 * ar - i0 * ai)
        pi.append(r0 * ai + i0 * ar)
    pwr = jnp.stack(pr, axis=1)
    pwi = jnp.stack(pi, axis=1)
    cr, ci = c_re.astype(f32), c_im.astype(f32)
    car = cr[:, None] * pwr[:, :t, None, :] - ci[:, None] * pwi[:, :t, None, :]
    cai = cr[:, None] * pwi[:, :t, None, :] + ci[:, None] * pwr[:, :t, None, :]
    kk = (jnp.einsum('gtnp,gpm->gtnm', car, bbr, precision=hp)
          - jnp.einsum('gtnp,gpm->gtnm', cai, bbi, precision=hp))
    ti = np.arange(t)
    sg = S5_SLAB_GROUPS
    ns = B_GROUPS // sg
    ch = B_GROUP_CH
    ns2 = 2 * B_STATE
    shift = np.zeros((t, ch, t, t, ch), np.float32)
    for t0 in range(t):
        for t1 in range(t0, t):
            shift[t1 - t0, :, t0, t1, :] = np.eye(ch)
    kmat = jnp.transpose(kk, (0, 3, 1, 2)).reshape(ns, sg * ch, t * ch)
    toep = jnp.einsum('srk,ktc->strc', _bf(kmat),
                      jnp.asarray(shift.reshape(t * ch, t, t * ch), jnp.bfloat16),
                      preferred_element_type=jnp.bfloat16)
    toep = toep.reshape(ns, t * LANES, t * ch)
    er = pwr[:, t - 1 - ti][:, :, None, :]
    ei = pwi[:, t - 1 - ti][:, :, None, :]
    bbr_t = jnp.transpose(bbr, (0, 2, 1))[:, None]
    bbi_t = jnp.transpose(bbi, (0, 2, 1))[:, None]
    w1 = jnp.concatenate([er * bbr_t - ei * bbi_t, er * bbi_t + ei * bbr_t], axis=-1)
    w1 = jnp.transpose(w1.reshape(ns, sg, t, ch, ns2), (0, 2, 1, 3, 4))
    w1 = _bf(w1.reshape(ns, t * LANES, ns2))
    zr1 = pwr[:, 1:, None, :]
    zi1 = pwi[:, 1:, None, :]
    w2r = cr[:, None] * zr1 - ci[:, None] * zi1
    w2i = -(cr[:, None] * zi1 + ci[:, None] * zr1)
    w2 = jnp.concatenate([w2r, w2i], axis=-1)
    w2 = jnp.transpose(w2.reshape(ns, sg, t, ch, ns2), (0, 1, 4, 2, 3))
    w2 = _bf(w2.reshape(ns, sg * ns2, t * ch))
    a1 = jnp.concatenate([pwr[:, t], pwr[:, t]], axis=-1)
    a2 = jnp.concatenate([-pwi[:, t], pwi[:, t]], axis=-1)
    dd = jnp.tile(d.astype(f32).reshape(ns, 1, LANES), (1, 1, t))
    return toep, w1, w2, a1, a2, dd


def _s5_spread_consts():
    t, sg, ch, ns2 = S5_CHUNK, S5_SLAB_GROUPS, B_GROUP_CH, 2 * B_STATE
    grp = np.arange(LANES) // ch
    col_un = np.kron(np.eye(t), np.kron(np.ones((1, sg)), np.eye(ch)))
    col_p = np.kron(np.ones((1, sg)), np.eye(ns2))
    keep_un = grp[:, None] == np.tile(grp, t)[None, :]
    keep_p = grp[:, None] == np.repeat(np.arange(sg), ns2)[None, :]
    keep_g = np.arange(sg)[:, None] == np.tile(grp, t)[None, :]
    bf = lambda m: jnp.asarray(m, jnp.bfloat16)
    f = lambda m: jnp.asarray(m, jnp.float32)
    return bf(col_un), bf(col_p), f(keep_un), f(keep_p), f(keep_g)


def _s5_rows(u_ref, nc):
    return jnp.concatenate([u_ref[pl.ds(t, nc, stride=S5_CHUNK), :] for t in range(S5_CHUNK)],
                           axis=1)


def _s5_local_kernel(u_ref, w1c_ref, colp_ref, keepp_ref, e_ref, w1_ref):
    @pl.when(pl.program_id(1) == 0)
    def _():
        for t in range(S5_CHUNK):
            rows = slice(t * LANES, (t + 1) * LANES)
            w1_ref[rows, :] = _bf(_dot(w1c_ref[0, rows, :], colp_ref[...]) * keepp_ref[...])

    e_ref[...] = _dot(_bf(_s5_rows(u_ref, e_ref.shape[0])), w1_ref[...])


def _s5_scan_kernel(e_ref, a1_ref, a2_ref, xp_ref, st_ref):
    @pl.when(pl.program_id(0) == 0)
    def _():
        st_ref[...] = jnp.zeros_like(st_ref)

    a1 = a1_ref[...]
    a2 = a2_ref[...]

    def step(c, carry):
        x, xs = carry
        xp_ref[c] = x
        e = e_ref[c]
        return a1 * x + a2 * xs + e, a1 * xs - a2 * x + pltpu.roll(e, B_STATE, 1)

    x0 = st_ref[...]
    x, _ = lax.fori_loop(0, e_ref.shape[0], step, (x0, pltpu.roll(x0, B_STATE, 1)), unroll=8)
    st_ref[...] = x


def _s5_out_kernel(u_ref, xp_ref, tc_ref, w2c_ref, colun_ref, keepun_ref, keepg_ref, d_ref,
                   y_ref, toep_ref, w2_ref):
    nc = xp_ref.shape[0]

    @pl.when(pl.program_id(1) == 0)
    def _():
        for t in range(S5_CHUNK):
            rows = slice(t * LANES, (t + 1) * LANES)
            toep_ref[rows, :] = _bf(_dot(tc_ref[0, rows, :], colun_ref[...]) * keepun_ref[...])
        for g in range(S5_SLAB_GROUPS):
            rows = slice(g * 2 * B_STATE, (g + 1) * 2 * B_STATE)
            w2_ref[rows, :] = _bf(_dot(w2c_ref[0, rows, :], colun_ref[...]) * keepg_ref[g:g + 1, :])

    x = _s5_rows(u_ref, nc)
    y = _dot(_bf(x), toep_ref[...]) + _dot(_bf(xp_ref[...]), w2_ref[...]) + d_ref[0] * x
    y = _gelu(y)
    for t in range(S5_CHUNK):
        y_ref[pl.ds(t, nc, stride=S5_CHUNK), :] = y[:, t * LANES:(t + 1) * LANES]


def _s5(proj, tables, cb, nrb):
    toep, w1, w2, a1, a2, dd = tables
    col_un, col_p, keep_un, keep_p, keep_g = _s5_spread_consts()
    n_tok = proj.shape[0]
    nc = n_tok // S5_CHUNK
    g = B_GROUPS
    ns = 2 * B_STATE
    n_slab = B_WIDTH // LANES
    sw = S5_SLAB_GROUPS * ns
    rw = S5_CHUNK * LANES
    u0 = 4 * A_WIDTH // LANES
    cb = min(cb, nc)
    ncb = nc // nrb
    tokb = n_tok // nrb
    fixed = lambda shape: pl.BlockSpec(shape, lambda s, r: (0,) * len(shape))
    slab = lambda shape: pl.BlockSpec((1,) + shape, lambda s, r: (s, 0, 0))
    e = pl.pallas_call(
        _s5_local_kernel,
        grid=(n_slab, nrb),
        in_specs=[pl.BlockSpec((tokb, LANES), lambda s, r: (r, u0 + s)),
                  slab(w1.shape[1:]), fixed(col_p.shape), fixed(keep_p.shape)],
        out_specs=pl.BlockSpec((ncb, sw), lambda s, r: (r, s)),
        out_shape=jax.ShapeDtypeStruct((nc, g * ns), jnp.float32),
        scratch_shapes=[pltpu.VMEM((rw, sw), jnp.bfloat16)],
        compiler_params=_cparams("parallel", "arbitrary"),
        name="s5_local",
    )(proj, w1, col_p, keep_p)
    xp = pl.pallas_call(
        _s5_scan_kernel,
        grid=(nc // cb,),
        in_specs=[pl.BlockSpec((cb, g, ns), lambda i: (i, 0, 0)),
                  pl.BlockSpec((g, ns), lambda i: (0, 0)),
                  pl.BlockSpec((g, ns), lambda i: (0, 0))],
        out_specs=pl.BlockSpec((cb, g, ns), lambda i: (i, 0, 0)),
        out_shape=jax.ShapeDtypeStruct((nc, g, ns), jnp.float32),
        scratch_shapes=[pltpu.VMEM((g, ns), jnp.float32)],
        compiler_params=_cparams("arbitrary"),
        name="s5_scan",
    )(e.reshape(nc, g, ns), a1, a2)
    return pl.pallas_call(
        _s5_out_kernel,
        grid=(n_slab, nrb),
        in_specs=[pl.BlockSpec((tokb, LANES), lambda s, r: (r, u0 + s)),
                  pl.BlockSpec((ncb, sw), lambda s, r: (r, s)),
                  slab(toep.shape[1:]), slab(w2.shape[1:]),
                  fixed(col_un.shape), fixed(keep_un.shape), fixed(keep_g.shape),
                  slab((1, rw))],
        out_specs=pl.BlockSpec((tokb, LANES), lambda s, r: (r, s)),
        out_shape=jax.ShapeDtypeStruct((n_tok, B_WIDTH), jnp.float32),
        scratch_shapes=[pltpu.VMEM((rw, rw), jnp.bfloat16),
                        pltpu.VMEM((sw, rw), jnp.bfloat16)],
        compiler_params=_cparams("parallel", "arbitrary"),
        name="s5_out",
    )(proj, xp.reshape(nc, g * ns), toep, w2, col_un, keep_un, keep_g, dd)


def _glu_kernel(y_ref, wa_ref, wb_ref, o_ref):
    y = _bf(y_ref[...])
    o_ref[...] = (_dot(y, wa_ref[...]) * _sigmoid(_dot(y, wb_ref[...]))).astype(o_ref.dtype)


def _glu(y, w, layer, tm, tn):
    m, k = y.shape
    n = w.shape[2] // 2
    tm = min(tm, m)
    nb = n // tn
    return pl.pallas_call(
        _glu_kernel,
        grid=(nb, m // tm),
        in_specs=[pl.BlockSpec((tm, k), lambda j, i: (i, 0)),
                  pl.BlockSpec((None, k, tn), lambda j, i: (layer, 0, j)),
                  pl.BlockSpec((None, k, tn), lambda j, i: (layer, 0, nb + j))],
        out_specs=pl.BlockSpec((tm, tn), lambda j, i: (i, j)),
        out_shape=jax.ShapeDtypeStruct((m, n), jnp.bfloat16),
        compiler_params=_cparams("parallel", "parallel"),
        name="glu",
    )(y, w, w)


def _up_merge_kernel(oa_ref, ob_ref, wa_ref, wb_ref, ga_ref, gb_ref, o_ref):
    m = (_sigmoid(ga_ref[...]) * _dot(oa_ref[...], wa_ref[...])
         + _sigmoid(gb_ref[...]) * _dot(ob_ref[...], wb_ref[...]))
    o_ref[...] = m.astype(o_ref.dtype)


def _up_merge(oa, ob, wa, wb, layer, proj, tm, tn):
    m, k = oa.shape
    n = wa.shape[2]
    tm = min(tm, m)
    ga0 = (4 * A_WIDTH + B_WIDTH) // tn
    gb0 = ga0 + D_MODEL // tn
    return pl.pallas_call(
        _up_merge_kernel,
        grid=(n // tn, m // tm),
        in_specs=[pl.BlockSpec((tm, k), lambda j, i: (i, 0)),
                  pl.BlockSpec((tm, k), lambda j, i: (i, 0)),
                  pl.BlockSpec((None, k, tn), lambda j, i: (layer, 0, j)),
                  pl.BlockSpec((None, k, tn), lambda j, i: (layer, 0, j)),
                  pl.BlockSpec((tm, tn), lambda j, i: (i, ga0 + j)),
                  pl.BlockSpec((tm, tn), lambda j, i: (i, gb0 + j))],
        out_specs=pl.BlockSpec((tm, tn), lambda j, i: (i, j)),
        out_shape=jax.ShapeDtypeStruct((m, n), jnp.bfloat16),
        compiler_params=_cparams("parallel", "parallel"),
        name="up_merge",
    )(oa, ob, wa, wb, proj, proj)


def _wo_ln_kernel(m_ref, w_ref, x_ref, g_ref, b_ref, o_ref, ob_ref):
    z = ALPHA * x_ref[...] + _dot(m_ref[...], w_ref[...])
    y = _layer_norm(z, g_ref[...], b_ref[...])
    o_ref[...] = y
    ob_ref[...] = _bf(y)


def _wo_ln(merged, w, layer, x, g, b, tm):
    m, k = merged.shape
    n = w.shape[2]
    tm = min(tm, m)
    row = lambda i: (i, 0)
    fix = lambda i: (0, 0)
    return pl.pallas_call(
        _wo_ln_kernel,
        grid=(m // tm,),
        in_specs=[pl.BlockSpec((tm, k), row), pl.BlockSpec((None, k, n), lambda i: (layer, 0, 0)),
                  pl.BlockSpec((tm, n), row), pl.BlockSpec((1, n), fix),
                  pl.BlockSpec((1, n), fix)],
        out_specs=[pl.BlockSpec((tm, n), row), pl.BlockSpec((tm, n), row)],
        out_shape=[jax.ShapeDtypeStruct((m, n), jnp.float32),
                   jax.ShapeDtypeStruct((m, n), jnp.bfloat16)],
        compiler_params=_cparams("parallel"),
        name="wo_ln",
    )(merged, w, x, g, b)


def _add_ln_kernel(x_ref, y_ref, g_ref, b_ref, o_ref, ob_ref):
    y = _layer_norm(ALPHA * x_ref[...] + y_ref[...], g_ref[...], b_ref[...])
    o_ref[...] = y
    ob_ref[...] = _bf(y)


def _add_ln(x, y, g, b, tm):
    m, n = x.shape
    tm = min(tm, m)
    row = lambda i: (i, 0)
    fix = lambda i: (0, 0)
    return pl.pallas_call(
        _add_ln_kernel,
        grid=(m // tm,),
        in_specs=[pl.BlockSpec((tm, n), row), pl.BlockSpec((tm, n), row),
                  pl.BlockSpec((1, n), fix), pl.BlockSpec((1, n), fix)],
        out_specs=[pl.BlockSpec((tm, n), row), pl.BlockSpec((tm, n), row)],
        out_shape=[jax.ShapeDtypeStruct((m, n), jnp.float32),
                   jax.ShapeDtypeStruct((m, n), jnp.bfloat16)],
        compiler_params=_cparams("parallel"),
        name="add_ln",
    )(x, y, g, b)


def _take_top(s, n_take, on_take):
    rows = s.shape[0]
    rid = lax.broadcasted_iota(jnp.int32, s.shape, 0)
    for k in range(n_take):
        m = jnp.max(s, axis=0, keepdims=True)
        idx = jnp.min(jnp.where(s == m, rid, rows), axis=0, keepdims=True)
        on_take(k, m, idx)
        s = jnp.where(rid == idx, -jnp.inf, s)


def _peer_route_kernel(q_ref, keys_ref, r2_ref, e2_ref, n_ref, c_ref):
    tt = q_ref.shape[0]
    kid = lax.broadcasted_iota(jnp.int32, (P_NKEYS, tt), 0)
    rank_id = lax.broadcasted_iota(jnp.int32, (P_TOPK, tt), 0)
    for h in range(P_HEADS):
        s = []
        for half in range(2):
            qh = q_ref[:, (2 * h + half) * P_HALF:(2 * h + half + 1) * P_HALF]
            s.append(lax.dot_general(_bf(keys_ref[h, half]), _bf(qh), _NT,
                                     preferred_element_type=jnp.float32))
        vals = [[], []]
        idxs = [[], []]
        rank2 = [jnp.full((P_NKEYS, tt), P_TOPK, jnp.int32)]
        for half in range(2):
            def take(k, m, idx, half=half):
                vals[half].append(m)
                idxs[half].append(idx)
                if half == 1:
                    rank2[0] = jnp.where(kid == idx, k, rank2[0])
            _take_top(s[half], P_TOPK, take)
        v1 = jnp.concatenate(vals[0], axis=0)
        v2 = jnp.concatenate(vals[1], axis=0)
        widths = [P_TOPK // (i + 1) for i in range(P_TOPK)]
        starts = np.cumsum([0] + widths)
        pad = int(-starts[-1] % 8)
        cand = jnp.concatenate([v1[i:i + 1] + v2[:widths[i]] for i in range(P_TOPK)]
                               + [jnp.full((pad, tt), -jnp.inf, jnp.float32)], axis=0)
        state = [jnp.zeros((P_TOPK, tt), jnp.int32), jnp.zeros((1, tt), jnp.float32)]
        top = v1[0:1] + v2[0:1]

        def take_c(k, m, pos):
            rank1 = sum((pos >= int(st)).astype(jnp.int32) for st in starts[1:P_TOPK])
            state[0] = state[0] + (rank_id == rank1).astype(jnp.int32)
            state[1] = state[1] + jnp.exp(m - top)
        _take_top(cand, P_TOPK, take_c)
        n_rank, z = state
        n_key = jnp.zeros((P_NKEYS, tt), jnp.int32)
        for i in range(P_TOPK):
            n_key = jnp.where(kid == idxs[0][i], n_rank[i:i + 1], n_key)
        r2_ref[h] = rank2[0].astype(jnp.float32).astype(r2_ref.dtype)
        n_ref[h] = n_key.astype(jnp.float32)
        e2_ref[h] = jnp.exp(s[1] - v2[0:1]).astype(e2_ref.dtype)
        c_ref[h] = jnp.exp(s[0] - v1[0:1]) / z


def _peer_route(q, keys, tt, gate_dtype=jnp.bfloat16):
    n_tok = q.shape[0]
    tt = min(tt, n_tok)
    out = [jax.ShapeDtypeStruct((P_HEADS, P_NKEYS, n_tok), dt)
           for dt in (gate_dtype, gate_dtype, jnp.float32, jnp.float32)]
    ospec = pl.BlockSpec((P_HEADS, P_NKEYS, tt), lambda i: (0, 0, i))
    return pl.pallas_call(
        _peer_route_kernel,
        grid=(n_tok // tt,),
        in_specs=[pl.BlockSpec((tt, P_HEADS * P_QDIM), lambda i: (i, 0)),
                  pl.BlockSpec((P_HEADS, 2, P_NKEYS, P_HALF), lambda i: (0, 0, 0, 0))],
        out_specs=[ospec] * 4,
        out_shape=out,
        compiler_params=_cparams("parallel"),
        name="peer_route",
    )(q, keys)


def _peer_ffn_kernel(xt_ref, u_ref, vt_ref, r2_ref, e2_ref, n_ref, c_ref, y_ref, acc_ref, h_ref):
    j = pl.program_id(1)
    eb = u_ref.shape[0]
    tm = xt_ref.shape[1]

    @pl.when(j == 0)
    def _():
        acc_ref[...] = jnp.zeros_like(acc_ref)

    act = _dot(u_ref[...], xt_ref[...])
    gdt = h_ref.dtype
    pack = 8 * 4 // jnp.dtype(gdt).itemsize
    for al in range(eb // P_NKEYS):
        a = j * (eb // P_NKEYS) + al
        rows = slice(al * P_NKEYS, (al + 1) * P_NKEYS)
        for cc in range(tm // GATE_COLS):
            cols = slice(cc * GATE_COLS, (cc + 1) * GATE_COLS)
            gate = None
            for h in range(P_HEADS):
                n_tile = jnp.broadcast_to(n_ref[h, pl.ds(a, 1), :][:, cols], (pack, GATE_COLS)).astype(gdt)
                c_tile = jnp.broadcast_to(c_ref[h, pl.ds(a, 1), :][:, cols], (pack, GATE_COLS)).astype(gdt)
                r2 = r2_ref[h, :, cols].reshape(P_NKEYS // pack, pack, GATE_COLS)
                e2 = e2_ref[h, :, cols].reshape(P_NKEYS // pack, pack, GATE_COLS)
                term = jnp.where(r2 < n_tile[None], e2 * c_tile[None], jnp.zeros((), gdt))
                gate = term if gate is None else gate + term
            g_act = _gelu(act[rows, cols].astype(gdt)).reshape(P_NKEYS // pack, pack, GATE_COLS)
            h_ref[rows, cols] = (g_act * gate).reshape(P_NKEYS, GATE_COLS)
    acc_ref[...] += _dot(vt_ref[...], h_ref[...])

    @pl.when(j == pl.num_programs(1) - 1)
    def _():
        y_ref[...] = acc_ref[...].T


def _peer_ffn(xt, u, vt, layer, route, tm, eb):
    d, n_tok = xt.shape
    tm = min(tm, n_tok)
    rspec = pl.BlockSpec((P_HEADS, P_NKEYS, tm), lambda i, j: (0, 0, i))
    return pl.pallas_call(
        _peer_ffn_kernel,
        grid=(n_tok // tm, P_NEXP // eb),
        in_specs=[pl.BlockSpec((d, tm), lambda i, j: (0, i)),
                  pl.BlockSpec((None, eb, d), lambda i, j: (layer, j, 0)),
                  pl.BlockSpec((None, d, eb), lambda i, j: (layer, 0, j)),
                  rspec, rspec, rspec, rspec],
        out_specs=pl.BlockSpec((tm, d), lambda i, j: (i, 0)),
        out_shape=jax.ShapeDtypeStruct((n_tok, d), jnp.float32),
        scratch_shapes=[pltpu.VMEM((d, tm), jnp.float32),
                        pltpu.VMEM((eb, tm), u.dtype)],
        compiler_params=_cparams("parallel", "arbitrary"),
        name="peer_ffn",
    )(xt, u, vt, *route)


def _chunk_tri():
    i = np.arange(LANES)
    same = (i[:, None] // HG_CHUNK) == (i[None, :] // HG_CHUNK)
    return jnp.asarray(same & (i[None, :] <= i[:, None]), jnp.bfloat16)


def kernel(x, w_in, hgrn_lb_logits, hgrn_norm_g, s5_lambda_re, s5_lambda_im, s5_log_step,
           s5_b_re, s5_b_im, s5_c_re, s5_c_im, s5_d, s5_w_glu, w_up_a, w_up_b, w_o,
           ln1_g, ln1_b, peer_w_q, peer_keys, peer_u, peer_v, ln2_g, ln2_b):
    bsz, n_tok, d = x.shape
    assert bsz == 1 and d == D_MODEL and n_tok % (S5_CHUNK * 8) == 0
    f32 = jnp.float32
    p = jax.nn.softmax(hgrn_lb_logits.astype(f32), axis=0)
    c = jnp.cumsum(p, axis=0)
    lbs = c - c[0:1]
    loglb = jnp.log(lbs)
    log1mlb = jnp.log1p(-lbs)
    tri = _chunk_tri()

    w_in_b, w_glu_b, w_ua_b, w_ub_b, w_o_b, w_q_b, u_b = (
        _bf(w) for w in (w_in, s5_w_glu, w_up_a, w_up_b, w_o, peer_w_q, peer_u))
    vt_b = _bf(jnp.swapaxes(peer_v, 1, 2))
    row = lambda p, l: p[l][None].astype(f32)

    xf = x.reshape(n_tok, d).astype(f32)
    xb = _bf(xf)
    for l in range(DEPTH):
        proj = _matmul(xb, w_in_b, l, f32, tm=MM_ROWS, tn=MM_COLS)
        oa = _hgrn2(proj, loglb[l:l + 1], log1mlb[l:l + 1], row(hgrn_norm_g, l), tri, tb=HG_ROWS)
        tables = _s5_tables(s5_lambda_re[l], s5_lambda_im[l], s5_log_step[l], s5_b_re[l],
                            s5_b_im[l], s5_c_re[l], s5_c_im[l], s5_d[l])
        yb = _s5(proj, tables, cb=S5_SCAN_CHUNKS, nrb=S5_ROW_BLOCKS)
        ob = _glu(yb, w_glu_b, l, tm=EPI_ROWS, tn=EPI_COLS)
        merged = _up_merge(oa, ob, w_ua_b, w_ub_b, l, proj, tm=EPI_ROWS, tn=EPI_COLS)
        xf, xb = _wo_ln(merged, w_o_b, l, xf, row(ln1_g, l), row(ln1_b, l), tm=LN_ROWS)
        q = _matmul(xb, w_q_b, l, jnp.bfloat16, tm=MM_ROWS, tn=MM_COLS)
        route = _peer_route(q, peer_keys[l].astype(f32), tt=ROUTE_TOKENS)
        y = _peer_ffn(xb.T, u_b, vt_b, l, route, tm=PEER_TOKENS, eb=PEER_EXPERTS)
        xf, xb = _add_ln(xf, y, row(ln2_g, l), row(ln2_b, l), tm=LN_ROWS)
    return xf.reshape(bsz, n_tok, d).astype(x.dtype)
```

```python
import math

import jax
import jax.numpy as jnp
import numpy as np
from jax import lax
from jax.experimental import pallas as pl
from jax.experimental.pallas import tpu as pltpu

D_MODEL = 2048
DEPTH = 4
A_HEADS = 8
A_DIM = 128
A_WIDTH = A_HEADS * A_DIM
HG_CHUNK = 16
HG_UNROLL = 8
B_GROUPS = 64
B_GROUP_CH = 16
B_STATE = 64
B_WIDTH = B_GROUPS * B_GROUP_CH
S5_CHUNK = 16
S5_SLAB_GROUPS = 8
S5_EIG_CLIP = -1e-4
IN_COLS = 4 * A_WIDTH + B_WIDTH + 2 * D_MODEL
P_HEADS = 8
P_QDIM = 256
P_HALF = 128
P_NKEYS = 128
P_NEXP = P_NKEYS * P_NKEYS
P_TOPK = 16
ALPHA = (2.0 * DEPTH) ** 0.25
LN_EPS = 1e-5
RMS_EPS = 1e-6

LANES = 128
GATE_COLS = 128
VMEM_LIMIT = 56 * 1024 * 1024

MM_ROWS, MM_COLS = 1024, 1024
EPI_ROWS, EPI_COLS = 1024, 512
LN_ROWS = 512
HG_ROWS = 1024
S5_SCAN_CHUNKS = 64
S5_ROW_BLOCKS = 2
ROUTE_TOKENS = 128
PEER_TOKENS, PEER_EXPERTS = 512, 1024

_NT = (((1,), (1,)), ((), ()))
_TN = (((0,), (0,)), ((), ()))


def _cparams(*sem):
    return pltpu.CompilerParams(dimension_semantics=sem, vmem_limit_bytes=VMEM_LIMIT)


def _bf(x):
    return x.astype(jnp.bfloat16)


def _dot(a, b):
    return jnp.dot(a, b, preferred_element_type=jnp.float32)


def _sigmoid(x):
    return 1.0 / (1.0 + jnp.exp(-x))


def _gelu(x):
    c = math.sqrt(2.0 / math.pi)
    return 0.5 * x * (1.0 + jnp.tanh(c * (x + 0.044715 * (x * x * x))))


def _layer_norm(z, g, b):
    mu = jnp.mean(z, axis=-1, keepdims=True)
    zc = z - mu
    var = jnp.mean(zc * zc, axis=-1, keepdims=True)
    return zc * lax.rsqrt(var + LN_EPS) * g + b


def _matmul_kernel(a_ref, w_ref, o_ref):
    o_ref[...] = _dot(a_ref[...], w_ref[...]).astype(o_ref.dtype)


def _matmul(a, w, layer, out_dtype, tm, tn):
    m, k = a.shape
    n = w.shape[2]
    tm = min(tm, m)
    return pl.pallas_call(
        _matmul_kernel,
        grid=(n // tn, m // tm),
        in_specs=[pl.BlockSpec((tm, k), lambda j, i: (i, 0)),
                  pl.BlockSpec((None, k, tn), lambda j, i: (layer, 0, j))],
        out_specs=pl.BlockSpec((tm, tn), lambda j, i: (i, j)),
        out_shape=jax.ShapeDtypeStruct((m, n), out_dtype),
        compiler_params=_cparams("parallel", "parallel"),
        name="matmul",
    )(a, w)


def _hgrn2_kernel(q_ref, f_ref, i_ref, g_ref, loglb_ref, log1mlb_ref, ng_ref, tri_ref,
                  o_ref, st_ref, b_ref, k_ref):
    tb = q_ref.shape[0]

    @pl.when(pl.program_id(1) == 0)
    def _():
        st_ref[...] = jnp.zeros_like(st_ref)

    fz = f_ref[...]
    log_sig = jnp.minimum(fz, 0.0) - jnp.log1p(jnp.exp(-jnp.abs(fz)))
    z = log1mlb_ref[...] + log_sig
    a = loglb_ref[...]
    logf = jnp.maximum(a, z) + jnp.log1p(jnp.exp(-jnp.abs(a - z)))
    k_ref[...] = 1.0 - jnp.exp(logf)
    tri = tri_ref[...]
    for r in range(tb // LANES):
        rows = slice(r * LANES, (r + 1) * LANES)
        lf = logf[rows]
        hi = _bf(lf)
        r1 = lf - hi.astype(jnp.float32)
        mid = _bf(r1)
        lo = _bf(r1 - mid.astype(jnp.float32))
        b_ref[rows, :] = _dot(tri, hi) + _dot(tri, mid) + _dot(tri, lo)

    srow = lax.broadcasted_iota(jnp.int32, (HG_CHUNK, 1), 0)
    ng = ng_ref[...]

    def chunk(j, st):
        rows = pl.ds(pl.multiple_of(j * HG_CHUNK, HG_CHUNK), HG_CHUNK)
        b = b_ref[rows, :]
        q = q_ref[rows, :]
        k = k_ref[rows, :]
        v = i_ref[rows, :]
        b_last = b[HG_CHUNK - 1:HG_CHUNK, :]
        o = lax.dot_general(_bf(q * jnp.exp(b)), _bf(st), _NT,
                            preferred_element_type=jnp.float32)
        intra = []
        for t in range(HG_CHUNK):
            diff = jnp.where(srow <= t, b[t:t + 1, :] - b, -jnp.inf)
            p = (q[t:t + 1, :] * k) * jnp.exp(diff)
            att = jnp.sum(p, axis=-1, keepdims=True)
            intra.append(jnp.sum(att * v, axis=0, keepdims=True))
        o = o + jnp.concatenate(intra, axis=0)
        k_tail = k * jnp.exp(b_last - b)
        st = st * jnp.exp(b_last) + lax.dot_general(
            _bf(v), _bf(k_tail), _TN, preferred_element_type=jnp.float32)
        o = o * lax.rsqrt(jnp.mean(o * o, axis=-1, keepdims=True) + RMS_EPS)
        o = o * ng * _sigmoid(g_ref[rows, :])
        o_ref[rows, :] = o.astype(o_ref.dtype)
        return st

    st_ref[...] = lax.fori_loop(0, tb // HG_CHUNK, chunk, st_ref[...], unroll=HG_UNROLL)


def _hgrn2(proj, loglb, log1mlb, norm_g, tri, tb):
    n_tok = proj.shape[0]
    tb = min(tb, n_tok)

    def col(off):
        return pl.BlockSpec((tb, A_DIM), lambda h, i: (i, off + h))

    def par():
        return pl.BlockSpec((1, A_DIM), lambda h, i: (0, h))

    return pl.pallas_call(
        _hgrn2_kernel,
        grid=(A_HEADS, n_tok // tb),
        in_specs=[col(0), col(A_HEADS), col(2 * A_HEADS), col(3 * A_HEADS),
                  par(), par(), par(),
                  pl.BlockSpec((LANES, LANES), lambda h, i: (0, 0))],
        out_specs=pl.BlockSpec((tb, A_DIM), lambda h, i: (i, h)),
        out_shape=jax.ShapeDtypeStruct((n_tok, A_WIDTH), jnp.bfloat16),
        scratch_shapes=[pltpu.VMEM((A_DIM, A_DIM), jnp.float32),
                        pltpu.VMEM((tb, A_DIM), jnp.float32),
                        pltpu.VMEM((tb, A_DIM), jnp.float32)],
        compiler_params=_cparams("parallel", "arbitrary"),
        name="hgrn2",
    )(proj, proj, proj, proj, loglb, log1mlb, norm_g, tri)


def _s5_tables(lam_re, lam_im, log_step, b_re, b_im, c_re, c_im, d):
    f32 = jnp.float32
    hp = lax.Precision.HIGHEST
    t = S5_CHUNK
    lr = jnp.minimum(lam_re.astype(f32), S5_EIG_CLIP)
    li = lam_im.astype(f32)
    dt = jnp.exp(log_step.astype(f32))[:, None]
    mag = jnp.exp(lr * dt)
    ar = mag * jnp.cos(li * dt)
    ai = mag * jnp.sin(li * dt)
    den = lr * lr + li * li
    nr = ar - 1.0
    zr = (nr * lr + ai * li) / den
    zi = (ai * lr - nr * li) / den
    br_, bi_ = b_re.astype(f32), b_im.astype(f32)
    bbr = zr[..., None] * br_ - zi[..., None] * bi_
    bbi = zr[..., None] * bi_ + zi[..., None] * br_
    pr = [jnp.ones_like(ar)]
    pi = [jnp.zeros_like(ai)]
    for _ in range(t):
        r0, i0 = pr[-1], pi[-1]
        pr.append(r0 * ar - i0 * ai)
        pi.append(r0 * ai + i0 * ar)
    pwr = jnp.stack(pr, axis=1)
    pwi = jnp.stack(pi, axis=1)
    cr, ci = c_re.astype(f32), c_im.astype(f32)
    car = cr[:, None] * pwr[:, :t, None, :] - ci[:, None] * pwi[:, :t, None, :]
    cai = cr[:, None] * pwi[:, :t, None, :] + ci[:, None] * pwr[:, :t, None, :]
    kk = (jnp.einsum('gtnp,gpm->gtnm', car, bbr, precision=hp)
          - jnp.einsum('gtnp,gpm->gtnm', cai, bbi, precision=hp))
    ti = np.arange(t)
    sg = S5_SLAB_GROUPS
    ns = B_GROUPS // sg
    ch = B_GROUP_CH
    ns2 = 2 * B_STATE
    shift = np.zeros((t, ch, t, t, ch), np.float32)
    for t0 in range(t):
        for t1 in range(t0, t):
            shift[t1 - t0, :, t0, t1, :] = np.eye(ch)
    kmat = jnp.transpose(kk, (0, 3, 1, 2)).reshape(ns, sg * ch, t * ch)
    toep = jnp.einsum('srk,ktc->strc', _bf(kmat),
                      jnp.asarray(shift.reshape(t * ch, t, t * ch), jnp.bfloat16),
                      preferred_element_type=jnp.bfloat16)
    toep = toep.reshape(ns, t * LANES, t * ch)
    er = pwr[:, t - 1 - ti][:, :, None, :]
    ei = pwi[:, t - 1 - ti][:, :, None, :]
    bbr_t = jnp.transpose(bbr, (0, 2, 1))[:, None]
    bbi_t = jnp.transpose(bbi, (0, 2, 1))[:, None]
    w1 = jnp.concatenate([er * bbr_t - ei * bbi_t, er * bbi_t + ei * bbr_t], axis=-1)
    w1 = jnp.transpose(w1.reshape(ns, sg, t, ch, ns2), (0, 2, 1, 3, 4))
    w1 = _bf(w1.reshape(ns, t * LANES, ns2))
    zr1 = pwr[:, 1:, None, :]
    zi1 = pwi[:, 1:, None, :]
    w2r = cr[:, None] * zr1 - ci[:, None] * zi1
    w2i = -(cr[:, None] * zi1 + ci[:, None] * zr1)
    w2 = jnp.concatenate([w2r, w2i], axis=-1)
    w2 = jnp.transpose(w2.reshape(ns, sg, t, ch, ns2), (0, 1, 4, 2, 3))
    w2 = _bf(w2.reshape(ns, sg * ns2, t * ch))
    a1 = jnp.concatenate([pwr[:, t], pwr[:, t]], axis=-1)
    a2 = jnp.concatenate([-pwi[:, t], pwi[:, t]], axis=-1)
    dd = jnp.tile(d.astype(f32).reshape(ns, 1, LANES), (1, 1, t))
    return toep, w1, w2, a1, a2, dd


def _s5_spread_consts():
    t, sg, ch, ns2 = S5_CHUNK, S5_SLAB_GROUPS, B_GROUP_CH, 2 * B_STATE
    grp = np.arange(LANES) // ch
    col_un = np.kron(np.eye(t), np.kron(np.ones((1, sg)), np.eye(ch)))
    col_p = np.kron(np.ones((1, sg)), np.eye(ns2))
    keep_un = grp[:, None] == np.tile(grp, t)[None, :]
    keep_p = grp[:, None] == np.repeat(np.arange(sg), ns2)[None, :]
    keep_g = np.arange(sg)[:, None] == np.tile(grp, t)[None, :]
    bf = lambda m: jnp.asarray(m, jnp.bfloat16)
    f = lambda m: jnp.asarray(m, jnp.float32)
    return bf(col_un), bf(col_p), f(keep_un), f(keep_p), f(keep_g)


def _s5_rows(u_ref, nc):
    return jnp.concatenate([u_ref[pl.ds(t, nc, stride=S5_CHUNK), :] for t in range(S5_CHUNK)],
                           axis=1)


def _s5_local_kernel(u_ref, w1c_ref, colp_ref, keepp_ref, e_ref, w1_ref):
    @pl.when(pl.program_id(1) == 0)
    def _():
        for t in range(S5_CHUNK):
            rows = slice(t * LANES, (t + 1) * LANES)
            w1_ref[rows, :] = _bf(_dot(w1c_ref[0, rows, :], colp_ref[...]) * keepp_ref[...])

    e_ref[...] = _dot(_bf(_s5_rows(u_ref, e_ref.shape[0])), w1_ref[...])


def _s5_scan_kernel(e_ref, a1_ref, a2_ref, xp_ref, st_ref):
    @pl.when(pl.program_id(0) == 0)
    def _():
        st_ref[...] = jnp.zeros_like(st_ref)

    a1 = a1_ref[...]
    a2 = a2_ref[...]

    def step(c, carry):
        x, xs = carry
        xp_ref[c] = x
        e = e_ref[c]
        return a1 * x + a2 * xs + e, a1 * xs - a2 * x + pltpu.roll(e, B_STATE, 1)

    x0 = st_ref[...]
    x, _ = lax.fori_loop(0, e_ref.shape[0], step, (x0, pltpu.roll(x0, B_STATE, 1)), unroll=8)
    st_ref[...] = x


def _s5_out_kernel(u_ref, xp_ref, tc_ref, w2c_ref, colun_ref, keepun_ref, keepg_ref, d_ref,
                   y_ref, toep_ref, w2_ref):
    nc = xp_ref.shape[0]

    @pl.when(pl.program_id(1) == 0)
    def _():
        for t in range(S5_CHUNK):
            rows = slice(t * LANES, (t + 1) * LANES)
            toep_ref[rows, :] = _bf(_dot(tc_ref[0, rows, :], colun_ref[...]) * keepun_ref[...])
        for g in range(S5_SLAB_GROUPS):
            rows = slice(g * 2 * B_STATE, (g + 1) * 2 * B_STATE)
            w2_ref[rows, :] = _bf(_dot(w2c_ref[0, rows, :], colun_ref[...]) * keepg_ref[g:g + 1, :])

    x = _s5_rows(u_ref, nc)
    y = _dot(_bf(x), toep_ref[...]) + _dot(_bf(xp_ref[...]), w2_ref[...]) + d_ref[0] * x
    y = _gelu(y)
    for t in range(S5_CHUNK):
        y_ref[pl.ds(t, nc, stride=S5_CHUNK), :] = y[:, t * LANES:(t + 1) * LANES]


def _s5(proj, tables, cb, nrb):
    toep, w1, w2, a1, a2, dd = tables
    col_un, col_p, keep_un, keep_p, keep_g = _s5_spread_consts()
    n_tok = proj.shape[0]
    nc = n_tok // S5_CHUNK
    g = B_GROUPS
    ns = 2 * B_STATE
    n_slab = B_WIDTH // LANES
    sw = S5_SLAB_GROUPS * ns
    rw = S5_CHUNK * LANES
    u0 = 4 * A_WIDTH // LANES
    cb = min(cb, nc)
    ncb = nc // nrb
    tokb = n_tok // nrb
    fixed = lambda shape: pl.BlockSpec(shape, lambda s, r: (0,) * len(shape))
    slab = lambda shape: pl.BlockSpec((1,) + shape, lambda s, r: (s, 0, 0))
    e = pl.pallas_call(
        _s5_local_kernel,
        grid=(n_slab, nrb),
        in_specs=[pl.BlockSpec((tokb, LANES), lambda s, r: (r, u0 + s)),
                  slab(w1.shape[1:]), fixed(col_p.shape), fixed(keep_p.shape)],
        out_specs=pl.BlockSpec((ncb, sw), lambda s, r: (r, s)),
        out_shape=jax.ShapeDtypeStruct((nc, g * ns), jnp.float32),
        scratch_shapes=[pltpu.VMEM((rw, sw), jnp.bfloat16)],
        compiler_params=_cparams("parallel", "arbitrary"),
        name="s5_local",
    )(proj, w1, col_p, keep_p)
    xp = pl.pallas_call(
        _s5_scan_kernel,
        grid=(nc // cb,),
        in_specs=[pl.BlockSpec((cb, g, ns), lambda i: (i, 0, 0)),
                  pl.BlockSpec((g, ns), lambda i: (0, 0)),
                  pl.BlockSpec((g, ns), lambda i: (0, 0))],
        out_specs=pl.BlockSpec((cb, g, ns), lambda i: (i, 0, 0)),
        out_shape=jax.ShapeDtypeStruct((nc, g, ns), jnp.float32),
        scratch_shapes=[pltpu.VMEM((g, ns), jnp.float32)],
        compiler_params=_cparams("arbitrary"),
        name="s5_scan",
    )(e.reshape(nc, g, ns), a1, a2)
    return pl.pallas_call(
        _s5_out_kernel,
        grid=(n_slab, nrb),
        in_specs=[pl.BlockSpec((tokb, LANES), lambda s, r: (r, u0 + s)),
                  pl.BlockSpec((ncb, sw), lambda s, r: (r, s)),
                  slab(toep.shape[1:]), slab(w2.shape[1:]),
                  fixed(col_un.shape), fixed(keep_un.shape), fixed(keep_g.shape),
                  slab((1, rw))],
        out_specs=pl.BlockSpec((tokb, LANES), lambda s, r: (r, s)),
        out_shape=jax.ShapeDtypeStruct((n_tok, B_WIDTH), jnp.float32),
        scratch_shapes=[pltpu.VMEM((rw, rw), jnp.bfloat16),
                        pltpu.VMEM((sw, rw), jnp.bfloat16)],
        compiler_params=_cparams("parallel", "arbitrary"),
        name="s5_out",
    )(proj, xp.reshape(nc, g * ns), toep, w2, col_un, keep_un, keep_g, dd)


def _glu_kernel(y_ref, wa_ref, wb_ref, o_ref):
    y = _bf(y_ref[...])
    o_ref[...] = (_dot(y, wa_ref[...]) * _sigmoid(_dot(y, wb_ref[...]))).astype(o_ref.dtype)


def _glu(y, w, layer, tm, tn):
    m, k = y.shape
    n = w.shape[2] // 2
    tm = min(tm, m)
    nb = n // tn
    return pl.pallas_call(
        _glu_kernel,
        grid=(nb, m // tm),
        in_specs=[pl.BlockSpec((tm, k), lambda j, i: (i, 0)),
                  pl.BlockSpec((None, k, tn), lambda j, i: (layer, 0, j)),
                  pl.BlockSpec((None, k, tn), lambda j, i: (layer, 0, nb + j))],
        out_specs=pl.BlockSpec((tm, tn), lambda j, i: (i, j)),
        out_shape=jax.ShapeDtypeStruct((m, n), jnp.bfloat16),
        compiler_params=_cparams("parallel", "parallel"),
        name="glu",
    )(y, w, w)


def _up_merge_kernel(oa_ref, ob_ref, wa_ref, wb_ref, ga_ref, gb_ref, o_ref):
    m = (_sigmoid(ga_ref[...]) * _dot(oa_ref[...], wa_ref[...])
         + _sigmoid(gb_ref[...]) * _dot(ob_ref[...], wb_ref[...]))
    o_ref[...] = m.astype(o_ref.dtype)


def _up_merge(oa, ob, wa, wb, layer, proj, tm, tn):
    m, k = oa.shape
    n = wa.shape[2]
    tm = min(tm, m)
    ga0 = (4 * A_WIDTH + B_WIDTH) // tn
    gb0 = ga0 + D_MODEL // tn
    return pl.pallas_call(
        _up_merge_kernel,
        grid=(n // tn, m // tm),
        in_specs=[pl.BlockSpec((tm, k), lambda j, i: (i, 0)),
                  pl.BlockSpec((tm, k), lambda j, i: (i, 0)),
                  pl.BlockSpec((None, k, tn), lambda j, i: (layer, 0, j)),
                  pl.BlockSpec((None, k, tn), lambda j, i: (layer, 0, j)),
                  pl.BlockSpec((tm, tn), lambda j, i: (i, ga0 + j)),
                  pl.BlockSpec((tm, tn), lambda j, i: (i, gb0 + j))],
        out_specs=pl.BlockSpec((tm, tn), lambda j, i: (i, j)),
        out_shape=jax.ShapeDtypeStruct((m, n), jnp.bfloat16),
        compiler_params=_cparams("parallel", "parallel"),
        name="up_merge",
    )(oa, ob, wa, wb, proj, proj)


def _wo_ln_kernel(m_ref, w_ref, x_ref, g_ref, b_ref, o_ref, ob_ref):
    z = ALPHA * x_ref[...] + _dot(m_ref[...], w_ref[...])
    y = _layer_norm(z, g_ref[...], b_ref[...])
    o_ref[...] = y
    ob_ref[...] = _bf(y)


def _wo_ln(merged, w, layer, x, g, b, tm):
    m, k = merged.shape
    n = w.shape[2]
    tm = min(tm, m)
    row = lambda i: (i, 0)
    fix = lambda i: (0, 0)
    return pl.pallas_call(
        _wo_ln_kernel,
        grid=(m // tm,),
        in_specs=[pl.BlockSpec((tm, k), row), pl.BlockSpec((None, k, n), lambda i: (layer, 0, 0)),
                  pl.BlockSpec((tm, n), row), pl.BlockSpec((1, n), fix),
                  pl.BlockSpec((1, n), fix)],
        out_specs=[pl.BlockSpec((tm, n), row), pl.BlockSpec((tm, n), row)],
        out_shape=[jax.ShapeDtypeStruct((m, n), jnp.float32),
                   jax.ShapeDtypeStruct((m, n), jnp.bfloat16)],
        compiler_params=_cparams("parallel"),
        name="wo_ln",
    )(merged, w, x, g, b)


def _take_top(s, n_take, on_take):
    rows = s.shape[0]
    rid = lax.broadcasted_iota(jnp.int32, s.shape, 0)
    for k in range(n_take):
        m = jnp.max(s, axis=0, keepdims=True)
        idx = jnp.min(jnp.where(s == m, rid, rows), axis=0, keepdims=True)
        on_take(k, m, idx)
        s = jnp.where(rid == idx, -jnp.inf, s)


def _peer_route_kernel(q_ref, keys_ref, r2_ref, e2_ref, n_ref, c_ref):
    tt = q_ref.shape[0]
    kid = lax.broadcasted_iota(jnp.int32, (P_NKEYS, tt), 0)
    rank_id = lax.broadcasted_iota(jnp.int32, (P_TOPK, tt), 0)
    for h in range(P_HEADS):
        s = []
        for half in range(2):
            qh = q_ref[:, (2 * h + half) * P_HALF:(2 * h + half + 1) * P_HALF]
            s.append(lax.dot_general(_bf(keys_ref[h, half]), _bf(qh), _NT,
                                     preferred_element_type=jnp.float32))
        vals = [[], []]
        idxs = [[], []]
        rank2 = [jnp.full((P_NKEYS, tt), P_TOPK, jnp.int32)]
        for half in range(2):
            def take(k, m, idx, half=half):
                vals[half].append(m)
                idxs[half].append(idx)
                if half == 1:
                    rank2[0] = jnp.where(kid == idx, k, rank2[0])
            _take_top(s[half], P_TOPK, take)
        v1 = jnp.concatenate(vals[0], axis=0)
        v2 = jnp.concatenate(vals[1], axis=0)
        widths = [P_TOPK // (i + 1) for i in range(P_TOPK)]
        starts = np.cumsum([0] + widths)
        pad = int(-starts[-1] % 8)
        cand = jnp.concatenate([v1[i:i + 1] + v2[:widths[i]] for i in range(P_TOPK)]
                               + [jnp.full((pad, tt), -jnp.inf, jnp.float32)], axis=0)
        state = [jnp.zeros((P_TOPK, tt), jnp.int32), jnp.zeros((1, tt), jnp.float32)]
        top = v1[0:1] + v2[0:1]

        def take_c(k, m, pos):
            rank1 = sum((pos >= int(st)).astype(jnp.int32) for st in starts[1:P_TOPK])
            state[0] = state[0] + (rank_id == rank1).astype(jnp.int32)
            state[1] = state[1] + jnp.exp(m - top)
        _take_top(cand, P_TOPK, take_c)
        n_rank, z = state
        n_key = jnp.zeros((P_NKEYS, tt), jnp.int32)
        for i in range(P_TOPK):
            n_key = jnp.where(kid == idxs[0][i], n_rank[i:i + 1], n_key)
        r2_ref[h] = rank2[0].astype(jnp.float32).astype(r2_ref.dtype)
        n_ref[h] = n_key.astype(jnp.float32)
        e2_ref[h] = jnp.exp(s[1] - v2[0:1]).astype(e2_ref.dtype)
        c_ref[h] = jnp.exp(s[0] - v1[0:1]) / z


def _peer_route(q, keys, tt, gate_dtype=jnp.bfloat16):
    n_tok = q.shape[0]
    tt = min(tt, n_tok)
    out = [jax.ShapeDtypeStruct((P_HEADS, P_NKEYS, n_tok), dt)
           for dt in (gate_dtype, gate_dtype, jnp.float32, jnp.float32)]
    ospec = pl.BlockSpec((P_HEADS, P_NKEYS, tt), lambda i: (0, 0, i))
    return pl.pallas_call(
        _peer_route_kernel,
        grid=(n_tok // tt,),
        in_specs=[pl.BlockSpec((tt, P_HEADS * P_QDIM), lambda i: (i, 0)),
                  pl.BlockSpec((P_HEADS, 2, P_NKEYS, P_HALF), lambda i: (0, 0, 0, 0))],
        out_specs=[ospec] * 4,
        out_shape=out,
        compiler_params=_cparams("parallel"),
        name="peer_route",
    )(q, keys)


def _peer_ffn_kernel(xt_ref, u_ref, vt_ref, r2_ref, e2_ref, n_ref, c_ref, x_ref, g_ref, b_ref,
                     o_ref, ob_ref, acc_ref, h_ref):
    j = pl.program_id(1)
    eb = u_ref.shape[0]
    tm = xt_ref.shape[1]

    @pl.when(j == 0)
    def _():
        acc_ref[...] = jnp.zeros_like(acc_ref)

    act = _dot(u_ref[...], xt_ref[...])
    gdt = h_ref.dtype
    pack = 8 * 4 // jnp.dtype(gdt).itemsize
    for al in range(eb // P_NKEYS):
        a = j * (eb // P_NKEYS) + al
        rows = slice(al * P_NKEYS, (al + 1) * P_NKEYS)
        for cc in range(tm // GATE_COLS):
            cols = slice(cc * GATE_COLS, (cc + 1) * GATE_COLS)
            gate = None
            for h in range(P_HEADS):
                n_tile = jnp.broadcast_to(n_ref[h, pl.ds(a, 1), :][:, cols], (pack, GATE_COLS)).astype(gdt)
                c_tile = jnp.broadcast_to(c_ref[h, pl.ds(a, 1), :][:, cols], (pack, GATE_COLS)).astype(gdt)
                r2 = r2_ref[h, :, cols].reshape(P_NKEYS // pack, pack, GATE_COLS)
                e2 = e2_ref[h, :, cols].reshape(P_NKEYS // pack, pack, GATE_COLS)
                term = jnp.where(r2 < n_tile[None], e2 * c_tile[None], jnp.zeros((), gdt))
                gate = term if gate is None else gate + term
            g_act = _gelu(act[rows, cols].astype(gdt)).reshape(P_NKEYS // pack, pack, GATE_COLS)
            h_ref[rows, cols] = (g_act * gate).reshape(P_NKEYS, GATE_COLS)
    acc_ref[...] += _dot(vt_ref[...], h_ref[...])

    @pl.when(j == pl.num_programs(1) - 1)
    def _():
        y = _layer_norm(ALPHA * x_ref[...] + acc_ref[...].T, g_ref[...], b_ref[...])
        o_ref[...] = y
        ob_ref[...] = _bf(y)


def _peer_ffn(xt, u, vt, layer, route, x, g, b, tm, eb):
    d, n_tok = xt.shape
    tm = min(tm, n_tok)
    once = pl.Buffered(1)
    rspec = pl.BlockSpec((P_HEADS, P_NKEYS, tm), lambda i, j: (0, 0, i), pipeline_mode=once)
    row = pl.BlockSpec((tm, d), lambda i, j: (i, 0))
    par = pl.BlockSpec((1, d), lambda i, j: (0, 0))
    return pl.pallas_call(
        _peer_ffn_kernel,
        grid=(n_tok // tm, P_NEXP // eb),
        in_specs=[pl.BlockSpec((d, tm), lambda i, j: (0, i), pipeline_mode=once),
                  pl.BlockSpec((None, eb, d), lambda i, j: (layer, j, 0)),
                  pl.BlockSpec((None, d, eb), lambda i, j: (layer, 0, j)),
                  rspec, rspec, rspec, rspec,
                  pl.BlockSpec((tm, d), lambda i, j: (i, 0), pipeline_mode=once), par, par],
        out_specs=[row, row],
        out_shape=[jax.ShapeDtypeStruct((n_tok, d), jnp.float32),
                   jax.ShapeDtypeStruct((n_tok, d), jnp.bfloat16)],
        scratch_shapes=[pltpu.VMEM((d, tm), jnp.float32),
                        pltpu.VMEM((eb, tm), u.dtype)],
        compiler_params=_cparams("parallel", "arbitrary"),
        name="peer_ffn",
    )(xt, u, vt, *route, x, g, b)


def _chunk_tri():
    i = np.arange(LANES)
    same = (i[:, None] // HG_CHUNK) == (i[None, :] // HG_CHUNK)
    return jnp.asarray(same & (i[None, :] <= i[:, None]), jnp.bfloat16)


def kernel(x, w_in, hgrn_lb_logits, hgrn_norm_g, s5_lambda_re, s5_lambda_im, s5_log_step,
           s5_b_re, s5_b_im, s5_c_re, s5_c_im, s5_d, s5_w_glu, w_up_a, w_up_b, w_o,
           ln1_g, ln1_b, peer_w_q, peer_keys, peer_u, peer_v, ln2_g, ln2_b):
    bsz, n_tok, d = x.shape
    assert bsz == 1 and d == D_MODEL and n_tok % (S5_CHUNK * 8) == 0
    f32 = jnp.float32
    p = jax.nn.softmax(hgrn_lb_logits.astype(f32), axis=0)
    c = jnp.cumsum(p, axis=0)
    lbs = c - c[0:1]
    loglb = jnp.log(lbs)
    log1mlb = jnp.log1p(-lbs)
    tri = _chunk_tri()

    w_in_b, w_glu_b, w_ua_b, w_ub_b, w_o_b, w_q_b, u_b = (
        _bf(w) for w in (w_in, s5_w_glu, w_up_a, w_up_b, w_o, peer_w_q, peer_u))
    vt_b = _bf(jnp.swapaxes(peer_v, 1, 2))
    row = lambda p, l: p[l][None].astype(f32)

    xf = x.reshape(n_tok, d).astype(f32)
    xb = _bf(xf)
    for l in range(DEPTH):
        proj = _matmul(xb, w_in_b, l, f32, tm=MM_ROWS, tn=MM_COLS)
        oa = _hgrn2(proj, loglb[l:l + 1], log1mlb[l:l + 1], row(hgrn_norm_g, l), tri, tb=HG_ROWS)
        tables = _s5_tables(s5_lambda_re[l], s5_lambda_im[l], s5_log_step[l], s5_b_re[l],
                            s5_b_im[l], s5_c_re[l], s5_c_im[l], s5_d[l])
        yb = _s5(proj, tables, cb=S5_SCAN_CHUNKS, nrb=S5_ROW_BLOCKS)
        ob = _glu(yb, w_glu_b, l, tm=EPI_ROWS, tn=EPI_COLS)
        merged = _up_merge(oa, ob, w_ua_b, w_ub_b, l, proj, tm=EPI_ROWS, tn=EPI_COLS)
        xf, xb = _wo_ln(merged, w_o_b, l, xf, row(ln1_g, l), row(ln1_b, l), tm=LN_ROWS)
        q = _matmul(xb, w_q_b, l, jnp.bfloat16, tm=MM_ROWS, tn=MM_COLS)
        route = _peer_route(q, peer_keys[l].astype(f32), tt=ROUTE_TOKENS)
        xf, xb = _peer_ffn(xb.T, u_b, vt_b, l, route, xf, row(ln2_g, l), row(ln2_b, l),
                           tm=PEER_TOKENS, eb=PEER_EXPERTS)
    return xf.reshape(bsz, n_tok, d).astype(x.dtype)
```

```python
import math

import jax
import jax.numpy as jnp
import numpy as np
from jax import lax
from jax.experimental import pallas as pl
from jax.experimental.pallas import tpu as pltpu

D_MODEL = 2048
DEPTH = 4
A_HEADS = 8
A_DIM = 128
A_WIDTH = A_HEADS * A_DIM
HG_CHUNK = 16
HG_UNROLL = 8
B_GROUPS = 64
B_GROUP_CH = 16
B_STATE = 64
B_WIDTH = B_GROUPS * B_GROUP_CH
S5_CHUNK = 16
S5_SLAB_GROUPS = 8
S5_EIG_CLIP = -1e-4
IN_COLS = 4 * A_WIDTH + B_WIDTH + 2 * D_MODEL
P_HEADS = 8
P_QDIM = 256
P_HALF = 128
P_NKEYS = 128
P_NEXP = P_NKEYS * P_NKEYS
P_TOPK = 16
ALPHA = (2.0 * DEPTH) ** 0.25
LN_EPS = 1e-5
RMS_EPS = 1e-6

LANES = 128
GATE_COLS = 128
VMEM_LIMIT = 52 * 1024 * 1024

MM_ROWS, MM_COLS = 1024, 1024
EPI_ROWS, EPI_COLS = 1024, 512
LN_ROWS = 512
HG_ROWS = 1024
S5_SCAN_CHUNKS = 64
S5_ROW_BLOCKS = 2
ROUTE_TOKENS = 128
PEER_TOKENS, PEER_EXPERTS = 512, 1024

_NT = (((1,), (1,)), ((), ()))
_TN = (((0,), (0,)), ((), ()))


def _cparams(*sem, fuse_inputs=None):
    return pltpu.CompilerParams(dimension_semantics=sem, vmem_limit_bytes=VMEM_LIMIT,
                                allow_input_fusion=fuse_inputs)


def _bf(x):
    return x.astype(jnp.bfloat16)


def _dot(a, b):
    return jnp.dot(a, b, preferred_element_type=jnp.float32)


def _sigmoid(x):
    return 1.0 / (1.0 + jnp.exp(-x))


def _gelu(x):
    c = math.sqrt(2.0 / math.pi)
    return 0.5 * x * (1.0 + jnp.tanh(c * (x + 0.044715 * (x * x * x))))


def _layer_norm(z, g, b):
    mu = jnp.mean(z, axis=-1, keepdims=True)
    zc = z - mu
    var = jnp.mean(zc * zc, axis=-1, keepdims=True)
    return zc * lax.rsqrt(var + LN_EPS) * g + b


def _matmul_kernel(a_ref, w_ref, o_ref):
    o_ref[...] = _dot(a_ref[...], w_ref[...]).astype(o_ref.dtype)


def _matmul(a, w, layer, out_dtype, tm, tn):
    m, k = a.shape
    n = w.shape[2]
    tm = min(tm, m)
    return pl.pallas_call(
        _matmul_kernel,
        grid=(n // tn, m // tm),
        in_specs=[pl.BlockSpec((tm, k), lambda j, i: (i, 0)),
                  pl.BlockSpec((None, k, tn), lambda j, i: (layer, 0, j))],
        out_specs=pl.BlockSpec((tm, tn), lambda j, i: (i, j)),
        out_shape=jax.ShapeDtypeStruct((m, n), out_dtype),
        compiler_params=_cparams("parallel", "parallel", fuse_inputs=[False, True]),
        name="matmul",
    )(a, w)


def _hgrn2_kernel(q_ref, f_ref, i_ref, g_ref, loglb_ref, log1mlb_ref, ng_ref, tri_ref,
                  o_ref, st_ref, b_ref, k_ref):
    tb = q_ref.shape[0]

    @pl.when(pl.program_id(1) == 0)
    def _():
        st_ref[...] = jnp.zeros_like(st_ref)

    fz = f_ref[...]
    log_sig = jnp.minimum(fz, 0.0) - jnp.log1p(jnp.exp(-jnp.abs(fz)))
    z = log1mlb_ref[...] + log_sig
    a = loglb_ref[...]
    logf = jnp.maximum(a, z) + jnp.log1p(jnp.exp(-jnp.abs(a - z)))
    k_ref[...] = 1.0 - jnp.exp(logf)
    tri = tri_ref[...]
    for r in range(tb // LANES):
        rows = slice(r * LANES, (r + 1) * LANES)
        lf = logf[rows]
        hi = _bf(lf)
        r1 = lf - hi.astype(jnp.float32)
        mid = _bf(r1)
        lo = _bf(r1 - mid.astype(jnp.float32))
        b_ref[rows, :] = _dot(tri, hi) + _dot(tri, mid) + _dot(tri, lo)

    srow = lax.broadcasted_iota(jnp.int32, (HG_CHUNK, 1), 0)
    ng = ng_ref[...]

    def chunk(j, st):
        rows = pl.ds(pl.multiple_of(j * HG_CHUNK, HG_CHUNK), HG_CHUNK)
        b = b_ref[rows, :]
        q = q_ref[rows, :]
        k = k_ref[rows, :]
        v = i_ref[rows, :]
        b_last = b[HG_CHUNK - 1:HG_CHUNK, :]
        o = lax.dot_general(_bf(q * jnp.exp(b)), _bf(st), _NT,
                            preferred_element_type=jnp.float32)
        intra = []
        for t in range(HG_CHUNK):
            diff = jnp.where(srow <= t, b[t:t + 1, :] - b, -jnp.inf)
            p = (q[t:t + 1, :] * k) * jnp.exp(diff)
            att = jnp.sum(p, axis=-1, keepdims=True)
            intra.append(jnp.sum(att * v, axis=0, keepdims=True))
        o = o + jnp.concatenate(intra, axis=0)
        k_tail = k * jnp.exp(b_last - b)
        st = st * jnp.exp(b_last) + lax.dot_general(
            _bf(v), _bf(k_tail), _TN, preferred_element_type=jnp.float32)
        o = o * lax.rsqrt(jnp.mean(o * o, axis=-1, keepdims=True) + RMS_EPS)
        o = o * ng * _sigmoid(g_ref[rows, :])
        o_ref[rows, :] = o.astype(o_ref.dtype)
        return st

    st_ref[...] = lax.fori_loop(0, tb // HG_CHUNK, chunk, st_ref[...], unroll=HG_UNROLL)


def _hgrn2(proj, loglb, log1mlb, norm_g, tri, tb):
    n_tok = proj.shape[0]
    tb = min(tb, n_tok)

    def col(off):
        return pl.BlockSpec((tb, A_DIM), lambda h, i: (i, off + h))

    def par():
        return pl.BlockSpec((1, A_DIM), lambda h, i: (0, h))

    return pl.pallas_call(
        _hgrn2_kernel,
        grid=(A_HEADS, n_tok // tb),
        in_specs=[col(0), col(A_HEADS), col(2 * A_HEADS), col(3 * A_HEADS),
                  par(), par(), par(),
                  pl.BlockSpec((LANES, LANES), lambda h, i: (0, 0))],
        out_specs=pl.BlockSpec((tb, A_DIM), lambda h, i: (i, h)),
        out_shape=jax.ShapeDtypeStruct((n_tok, A_WIDTH), jnp.bfloat16),
        scratch_shapes=[pltpu.VMEM((A_DIM, A_DIM), jnp.float32),
                        pltpu.VMEM((tb, A_DIM), jnp.float32),
                        pltpu.VMEM((tb, A_DIM), jnp.float32)],
        compiler_params=_cparams("parallel", "arbitrary"),
        name="hgrn2",
    )(proj, proj, proj, proj, loglb, log1mlb, norm_g, tri)


def _s5_tables(lam_re, lam_im, log_step, b_re, b_im, c_re, c_im, d):
    f32 = jnp.float32
    hp = lax.Precision.HIGHEST
    t = S5_CHUNK
    lr = jnp.minimum(lam_re.astype(f32), S5_EIG_CLIP)
    li = lam_im.astype(f32)
    dt = jnp.exp(log_step.astype(f32))[:, None]
    mag = jnp.exp(lr * dt)
    ar = mag * jnp.cos(li * dt)
    ai = mag * jnp.sin(li * dt)
    den = lr * lr + li * li
    nr = ar - 1.0
    zr = (nr * lr + ai * li) / den
    zi = (ai * lr - nr * li) / den
    br_, bi_ = b_re.astype(f32), b_im.astype(f32)
    bbr = zr[..., None] * br_ - zi[..., None] * bi_
    bbi = zr[..., None] * bi_ + zi[..., None] * br_
    pr = [jnp.ones_like(ar)]
    pi = [jnp.zeros_like(ai)]
    for _ in range(t):
        r0, i0 = pr[-1], pi[-1]
        pr.append(r0 * ar - i0 * ai)
        pi.append(r0 * ai + i0 * ar)
    pwr = jnp.stack(pr, axis=1)
    pwi = jnp.stack(pi, axis=1)
    cr, ci = c_re.astype(f32), c_im.astype(f32)
    car = cr[:, None] * pwr[:, :t, None, :] - ci[:, None] * pwi[:, :t, None, :]
    cai = cr[:, None] * pwi[:, :t, None, :] + ci[:, None] * pwr[:, :t, None, :]
    kk = (jnp.einsum('gtnp,gpm->gtnm', car, bbr, precision=hp)
          - jnp.einsum('gtnp,gpm->gtnm', cai, bbi, precision=hp))
    ti = np.arange(t)
    sg = S5_SLAB_GROUPS
    ns = B_GROUPS // sg
    ch = B_GROUP_CH
    ns2 = 2 * B_STATE
    shift = np.zeros((t, ch, t, t, ch), np.float32)
    for t0 in range(t):
        for t1 in range(t0, t):
            shift[t1 - t0, :, t0, t1, :] = np.eye(ch)
    kmat = jnp.transpose(kk, (0, 3, 1, 2)).reshape(ns, sg * ch, t * ch)
    toep = jnp.einsum('srk,ktc->strc', _bf(kmat),
                      jnp.asarray(shift.reshape(t * ch, t, t * ch), jnp.bfloat16),
                      preferred_element_type=jnp.bfloat16)
    toep = toep.reshape(ns, t * LANES, t * ch)
    er = pwr[:, t - 1 - ti][:, :, None, :]
    ei = pwi[:, t - 1 - ti][:, :, None, :]
    bbr_t = jnp.transpose(bbr, (0, 2, 1))[:, None]
    bbi_t = jnp.transpose(bbi, (0, 2, 1))[:, None]
    w1 = jnp.concatenate([er * bbr_t - ei * bbi_t, er * bbi_t + ei * bbr_t], axis=-1)
    w1 = jnp.transpose(w1.reshape(ns, sg, t, ch, ns2), (0, 2, 1, 3, 4))
    w1 = _bf(w1.reshape(ns, t * LANES, ns2))
    zr1 = pwr[:, 1:, None, :]
    zi1 = pwi[:, 1:, None, :]
    w2r = cr[:, None] * zr1 - ci[:, None] * zi1
    w2i = -(cr[:, None] * zi1 + ci[:, None] * zr1)
    w2 = jnp.concatenate([w2r, w2i], axis=-1)
    w2 = jnp.transpose(w2.reshape(ns, sg, t, ch, ns2), (0, 1, 4, 2, 3))
    w2 = _bf(w2.reshape(ns, sg * ns2, t * ch))
    a1 = jnp.concatenate([pwr[:, t], pwr[:, t]], axis=-1)
    a2 = jnp.concatenate([-pwi[:, t], pwi[:, t]], axis=-1)
    dd = jnp.tile(d.astype(f32).reshape(ns, 1, LANES), (1, 1, t))
    return toep, w1, w2, a1, a2, dd


def _s5_spread_consts():
    t, sg, ch, ns2 = S5_CHUNK, S5_SLAB_GROUPS, B_GROUP_CH, 2 * B_STATE
    grp = np.arange(LANES) // ch
    col_un = np.kron(np.eye(t), np.kron(np.ones((1, sg)), np.eye(ch)))
    col_p = np.kron(np.ones((1, sg)), np.eye(ns2))
    keep_un = grp[:, None] == np.tile(grp, t)[None, :]
    keep_p = grp[:, None] == np.repeat(np.arange(sg), ns2)[None, :]
    keep_g = np.arange(sg)[:, None] == np.tile(grp, t)[None, :]
    bf = lambda m: jnp.asarray(m, jnp.bfloat16)
    f = lambda m: jnp.asarray(m, jnp.float32)
    return bf(col_un), bf(col_p), f(keep_un), f(keep_p), f(keep_g)


def _s5_rows(u_ref, nc):
    return jnp.concatenate([u_ref[pl.ds(t, nc, stride=S5_CHUNK), :] for t in range(S5_CHUNK)],
                           axis=1)


def _s5_local_kernel(u_ref, w1c_ref, colp_ref, keepp_ref, e_ref, w1_ref):
    @pl.when(pl.program_id(1) == 0)
    def _():
        for t in range(S5_CHUNK):
            rows = slice(t * LANES, (t + 1) * LANES)
            w1_ref[rows, :] = _bf(_dot(w1c_ref[0, rows, :], colp_ref[...]) * keepp_ref[...])

    e_ref[...] = _dot(_bf(_s5_rows(u_ref, e_ref.shape[0])), w1_ref[...])


def _s5_scan_kernel(e_ref, a1_ref, a2_ref, xp_ref, st_ref):
    @pl.when(pl.program_id(0) == 0)
    def _():
        st_ref[...] = jnp.zeros_like(st_ref)

    a1 = a1_ref[...]
    a2 = a2_ref[...]

    def step(c, carry):
        x, xs = carry
        xp_ref[c] = x
        e = e_ref[c]
        return a1 * x + a2 * xs + e, a1 * xs - a2 * x + pltpu.roll(e, B_STATE, 1)

    x0 = st_ref[...]
    x, _ = lax.fori_loop(0, e_ref.shape[0], step, (x0, pltpu.roll(x0, B_STATE, 1)), unroll=8)
    st_ref[...] = x


def _s5_out_kernel(u_ref, xp_ref, tc_ref, w2c_ref, colun_ref, keepun_ref, keepg_ref, d_ref,
                   y_ref, toep_ref, w2_ref):
    nc = xp_ref.shape[0]

    @pl.when(pl.program_id(1) == 0)
    def _():
        for t in range(S5_CHUNK):
            rows = slice(t * LANES, (t + 1) * LANES)
            toep_ref[rows, :] = _bf(_dot(tc_ref[0, rows, :], colun_ref[...]) * keepun_ref[...])
        for g in range(S5_SLAB_GROUPS):
            rows = slice(g * 2 * B_STATE, (g + 1) * 2 * B_STATE)
            w2_ref[rows, :] = _bf(_dot(w2c_ref[0, rows, :], colun_ref[...]) * keepg_ref[g:g + 1, :])

    x = _s5_rows(u_ref, nc)
    y = _dot(_bf(x), toep_ref[...]) + _dot(_bf(xp_ref[...]), w2_ref[...]) + d_ref[0] * x
    y = _gelu(y)
    for t in range(S5_CHUNK):
        y_ref[pl.ds(t, nc, stride=S5_CHUNK), :] = y[:, t * LANES:(t + 1) * LANES]


def _s5(proj, tables, cb, nrb):
    toep, w1, w2, a1, a2, dd = tables
    col_un, col_p, keep_un, keep_p, keep_g = _s5_spread_consts()
    n_tok = proj.shape[0]
    nc = n_tok // S5_CHUNK
    g = B_GROUPS
    ns = 2 * B_STATE
    n_slab = B_WIDTH // LANES
    sw = S5_SLAB_GROUPS * ns
    rw = S5_CHUNK * LANES
    u0 = 4 * A_WIDTH // LANES
    cb = min(cb, nc)
    ncb = nc // nrb
    tokb = n_tok // nrb
    fixed = lambda shape: pl.BlockSpec(shape, lambda s, r: (0,) * len(shape))
    slab = lambda shape: pl.BlockSpec((1,) + shape, lambda s, r: (s, 0, 0))
    e = pl.pallas_call(
        _s5_local_kernel,
        grid=(n_slab, nrb),
        in_specs=[pl.BlockSpec((tokb, LANES), lambda s, r: (r, u0 + s)),
                  slab(w1.shape[1:]), fixed(col_p.shape), fixed(keep_p.shape)],
        out_specs=pl.BlockSpec((ncb, sw), lambda s, r: (r, s)),
        out_shape=jax.ShapeDtypeStruct((nc, g * ns), jnp.float32),
        scratch_shapes=[pltpu.VMEM((rw, sw), jnp.bfloat16)],
        compiler_params=_cparams("parallel", "arbitrary"),
        name="s5_local",
    )(proj, w1, col_p, keep_p)
    xp = pl.pallas_call(
        _s5_scan_kernel,
        grid=(nc // cb,),
        in_specs=[pl.BlockSpec((cb, g, ns), lambda i: (i, 0, 0)),
                  pl.BlockSpec((g, ns), lambda i: (0, 0)),
                  pl.BlockSpec((g, ns), lambda i: (0, 0))],
        out_specs=pl.BlockSpec((cb, g, ns), lambda i: (i, 0, 0)),
        out_shape=jax.ShapeDtypeStruct((nc, g, ns), jnp.float32),
        scratch_shapes=[pltpu.VMEM((g, ns), jnp.float32)],
        compiler_params=_cparams("arbitrary"),
        name="s5_scan",
    )(e.reshape(nc, g, ns), a1, a2)
    return pl.pallas_call(
        _s5_out_kernel,
        grid=(n_slab, nrb),
        in_specs=[pl.BlockSpec((tokb, LANES), lambda s, r: (r, u0 + s)),
                  pl.BlockSpec((ncb, sw), lambda s, r: (r, s)),
                  slab(toep.shape[1:]), slab(w2.shape[1:]),
                  fixed(col_un.shape), fixed(keep_un.shape), fixed(keep_g.shape),
                  slab((1, rw))],
        out_specs=pl.BlockSpec((tokb, LANES), lambda s, r: (r, s)),
        out_shape=jax.ShapeDtypeStruct((n_tok, B_WIDTH), jnp.float32),
        scratch_shapes=[pltpu.VMEM((rw, rw), jnp.bfloat16),
                        pltpu.VMEM((sw, rw), jnp.bfloat16)],
        compiler_params=_cparams("parallel", "arbitrary"),
        name="s5_out",
    )(proj, xp.reshape(nc, g * ns), toep, w2, col_un, keep_un, keep_g, dd)


def _glu_kernel(y_ref, wa_ref, wb_ref, o_ref):
    y = _bf(y_ref[...])
    o_ref[...] = (_dot(y, wa_ref[...]) * _sigmoid(_dot(y, wb_ref[...]))).astype(o_ref.dtype)


def _glu(y, w, layer, tm, tn):
    m, k = y.shape
    n = w.shape[2] // 2
    tm = min(tm, m)
    nb = n // tn
    return pl.pallas_call(
        _glu_kernel,
        grid=(nb, m // tm),
        in_specs=[pl.BlockSpec((tm, k), lambda j, i: (i, 0)),
                  pl.BlockSpec((None, k, tn), lambda j, i: (layer, 0, j)),
                  pl.BlockSpec((None, k, tn), lambda j, i: (layer, 0, nb + j))],
        out_specs=pl.BlockSpec((tm, tn), lambda j, i: (i, j)),
        out_shape=jax.ShapeDtypeStruct((m, n), jnp.bfloat16),
        compiler_params=_cparams("parallel", "parallel", fuse_inputs=[False, True, True]),
        name="glu",
    )(y, w, w)


def _up_merge_kernel(oa_ref, ob_ref, wa_ref, wb_ref, ga_ref, gb_ref, o_ref):
    m = (_sigmoid(ga_ref[...]) * _dot(oa_ref[...], wa_ref[...])
         + _sigmoid(gb_ref[...]) * _dot(ob_ref[...], wb_ref[...]))
    o_ref[...] = m.astype(o_ref.dtype)


def _up_merge(oa, ob, wa, wb, layer, proj, tm, tn):
    m, k = oa.shape
    n = wa.shape[2]
    tm = min(tm, m)
    ga0 = (4 * A_WIDTH + B_WIDTH) // tn
    gb0 = ga0 + D_MODEL // tn
    return pl.pallas_call(
        _up_merge_kernel,
        grid=(n // tn, m // tm),
        in_specs=[pl.BlockSpec((tm, k), lambda j, i: (i, 0)),
                  pl.BlockSpec((tm, k), lambda j, i: (i, 0)),
                  pl.BlockSpec((None, k, tn), lambda j, i: (layer, 0, j)),
                  pl.BlockSpec((None, k, tn), lambda j, i: (layer, 0, j)),
                  pl.BlockSpec((tm, tn), lambda j, i: (i, ga0 + j)),
                  pl.BlockSpec((tm, tn), lambda j, i: (i, gb0 + j))],
        out_specs=pl.BlockSpec((tm, tn), lambda j, i: (i, j)),
        out_shape=jax.ShapeDtypeStruct((m, n), jnp.bfloat16),
        compiler_params=_cparams("parallel", "parallel",
                                 fuse_inputs=[False, False, True, True, False, False]),
        name="up_merge",
    )(oa, ob, wa, wb, proj, proj)


def _wo_ln_kernel(m_ref, w_ref, x_ref, g_ref, b_ref, o_ref, ob_ref):
    z = ALPHA * x_ref[...] + _dot(m_ref[...], w_ref[...])
    y = _layer_norm(z, g_ref[...], b_ref[...])
    o_ref[...] = y
    ob_ref[...] = _bf(y)


def _wo_ln(merged, w, layer, x, g, b, tm):
    m, k = merged.shape
    n = w.shape[2]
    tm = min(tm, m)
    row = lambda i: (i, 0)
    fix = lambda i: (0, 0)
    return pl.pallas_call(
        _wo_ln_kernel,
        grid=(m // tm,),
        in_specs=[pl.BlockSpec((tm, k), row), pl.BlockSpec((None, k, n), lambda i: (layer, 0, 0)),
                  pl.BlockSpec((tm, n), row), pl.BlockSpec((1, n), fix),
                  pl.BlockSpec((1, n), fix)],
        out_specs=[pl.BlockSpec((tm, n), row), pl.BlockSpec((tm, n), row)],
        out_shape=[jax.ShapeDtypeStruct((m, n), jnp.float32),
                   jax.ShapeDtypeStruct((m, n), jnp.bfloat16)],
        compiler_params=_cparams("parallel", fuse_inputs=[False, True, False, False, False]),
        name="wo_ln",
    )(merged, w, x, g, b)


def _add_ln_kernel(x_ref, y_ref, g_ref, b_ref, o_ref, ob_ref):
    y = _layer_norm(ALPHA * x_ref[...] + y_ref[...], g_ref[...], b_ref[...])
    o_ref[...] = y
    ob_ref[...] = _bf(y)


def _add_ln(x, y, g, b, tm):
    m, n = x.shape
    tm = min(tm, m)
    row = lambda i: (i, 0)
    fix = lambda i: (0, 0)
    return pl.pallas_call(
        _add_ln_kernel,
        grid=(m // tm,),
        in_specs=[pl.BlockSpec((tm, n), row), pl.BlockSpec((tm, n), row),
                  pl.BlockSpec((1, n), fix), pl.BlockSpec((1, n), fix)],
        out_specs=[pl.BlockSpec((tm, n), row), pl.BlockSpec((tm, n), row)],
        out_shape=[jax.ShapeDtypeStruct((m, n), jnp.float32),
                   jax.ShapeDtypeStruct((m, n), jnp.bfloat16)],
        compiler_params=_cparams("parallel"),
        name="add_ln",
    )(x, y, g, b)


def _take_top(s, n_take, on_take):
    rows = s.shape[0]
    rid = lax.broadcasted_iota(jnp.int32, s.shape, 0)
    for k in range(n_take):
        m = jnp.max(s, axis=0, keepdims=True)
        idx = jnp.min(jnp.where(s == m, rid, rows), axis=0, keepdims=True)
        on_take(k, m, idx)
        s = jnp.where(rid == idx, -jnp.inf, s)


def _peer_route_kernel(q_ref, keys_ref, r2_ref, e2_ref, n_ref, c_ref):
    tt = q_ref.shape[0]
    kid = lax.broadcasted_iota(jnp.int32, (P_NKEYS, tt), 0)
    rank_id = lax.broadcasted_iota(jnp.int32, (P_TOPK, tt), 0)
    for h in range(P_HEADS):
        s = []
        for half in range(2):
            qh = q_ref[:, (2 * h + half) * P_HALF:(2 * h + half + 1) * P_HALF]
            s.append(lax.dot_general(_bf(keys_ref[h, half]), _bf(qh), _NT,
                                     preferred_element_type=jnp.float32))
        vals = [[], []]
        idxs = [[], []]
        rank2 = [jnp.full((P_NKEYS, tt), P_TOPK, jnp.int32)]
        for half in range(2):
            def take(k, m, idx, half=half):
                vals[half].append(m)
                idxs[half].append(idx)
                if half == 1:
                    rank2[0] = jnp.where(kid == idx, k, rank2[0])
            _take_top(s[half], P_TOPK, take)
        v1 = jnp.concatenate(vals[0], axis=0)
        v2 = jnp.concatenate(vals[1], axis=0)
        widths = [P_TOPK // (i + 1) for i in range(P_TOPK)]
        starts = np.cumsum([0] + widths)
        pad = int(-starts[-1] % 8)
        cand = jnp.concatenate([v1[i:i + 1] + v2[:widths[i]] for i in range(P_TOPK)]
                               + [jnp.full((pad, tt), -jnp.inf, jnp.float32)], axis=0)
        state = [jnp.zeros((P_TOPK, tt), jnp.int32), jnp.zeros((1, tt), jnp.float32)]
        top = v1[0:1] + v2[0:1]

        def take_c(k, m, pos):
            rank1 = sum((pos >= int(st)).astype(jnp.int32) for st in starts[1:P_TOPK])
            state[0] = state[0] + (rank_id == rank1).astype(jnp.int32)
            state[1] = state[1] + jnp.exp(m - top)
        _take_top(cand, P_TOPK, take_c)
        n_rank, z = state
        n_key = jnp.zeros((P_NKEYS, tt), jnp.int32)
        for i in range(P_TOPK):
            n_key = jnp.where(kid == idxs[0][i], n_rank[i:i + 1], n_key)
        r2_ref[h] = rank2[0].astype(jnp.float32).astype(r2_ref.dtype)
        n_ref[h] = n_key.astype(jnp.float32)
        e2_ref[h] = jnp.exp(s[1] - v2[0:1]).astype(e2_ref.dtype)
        c_ref[h] = jnp.exp(s[0] - v1[0:1]) / z


def _peer_route(q, keys, tt, gate_dtype=jnp.bfloat16):
    n_tok = q.shape[0]
    tt = min(tt, n_tok)
    out = [jax.ShapeDtypeStruct((P_HEADS, P_NKEYS, n_tok), dt)
           for dt in (gate_dtype, gate_dtype, jnp.float32, jnp.float32)]
    ospec = pl.BlockSpec((P_HEADS, P_NKEYS, tt), lambda i: (0, 0, i))
    return pl.pallas_call(
        _peer_route_kernel,
        grid=(n_tok // tt,),
        in_specs=[pl.BlockSpec((tt, P_HEADS * P_QDIM), lambda i: (i, 0)),
                  pl.BlockSpec((P_HEADS, 2, P_NKEYS, P_HALF), lambda i: (0, 0, 0, 0))],
        out_specs=[ospec] * 4,
        out_shape=out,
        compiler_params=_cparams("parallel"),
        name="peer_route",
    )(q, keys)


def _peer_ffn_kernel(xt_ref, u_ref, vt_ref, r2_ref, e2_ref, n_ref, c_ref, y_ref, acc_ref, h_ref):
    j = pl.program_id(1)
    eb = u_ref.shape[0]
    tm = xt_ref.shape[1]

    @pl.when(j == 0)
    def _():
        acc_ref[...] = jnp.zeros_like(acc_ref)

    act = _dot(u_ref[...], xt_ref[...])
    gdt = h_ref.dtype
    pack = 8 * 4 // jnp.dtype(gdt).itemsize
    for al in range(eb // P_NKEYS):
        a = j * (eb // P_NKEYS) + al
        rows = slice(al * P_NKEYS, (al + 1) * P_NKEYS)
        for cc in range(tm // GATE_COLS):
            cols = slice(cc * GATE_COLS, (cc + 1) * GATE_COLS)
            gate = None
            for h in range(P_HEADS):
                n_tile = jnp.broadcast_to(n_ref[h, pl.ds(a, 1), :][:, cols], (pack, GATE_COLS)).astype(gdt)
                c_tile = jnp.broadcast_to(c_ref[h, pl.ds(a, 1), :][:, cols], (pack, GATE_COLS)).astype(gdt)
                r2 = r2_ref[h, :, cols].reshape(P_NKEYS // pack, pack, GATE_COLS)
                e2 = e2_ref[h, :, cols].reshape(P_NKEYS // pack, pack, GATE_COLS)
                term = jnp.where(r2 < n_tile[None], e2 * c_tile[None], jnp.zeros((), gdt))
                gate = term if gate is None else gate + term
            g_act = _gelu(act[rows, cols].astype(gdt)).reshape(P_NKEYS // pack, pack, GATE_COLS)
            h_ref[rows, cols] = (g_act * gate).reshape(P_NKEYS, GATE_COLS)
    acc_ref[...] += _dot(vt_ref[...], h_ref[...])

    @pl.when(j == pl.num_programs(1) - 1)
    def _():
        y_ref[...] = acc_ref[...].T


def _peer_ffn(xt, u, vt, layer, route, tm, eb):
    d, n_tok = xt.shape
    tm = min(tm, n_tok)
    rspec = pl.BlockSpec((P_HEADS, P_NKEYS, tm), lambda i, j: (0, 0, i))
    return pl.pallas_call(
        _peer_ffn_kernel,
        grid=(n_tok // tm, P_NEXP // eb),
        in_specs=[pl.BlockSpec((d, tm), lambda i, j: (0, i)),
                  pl.BlockSpec((None, eb, d), lambda i, j: (layer, j, 0)),
                  pl.BlockSpec((None, d, eb), lambda i, j: (layer, 0, j)),
                  rspec, rspec, rspec, rspec],
        out_specs=pl.BlockSpec((tm, d), lambda i, j: (i, 0)),
        out_shape=jax.ShapeDtypeStruct((n_tok, d), jnp.float32),
        scratch_shapes=[pltpu.VMEM((d, tm), jnp.float32),
                        pltpu.VMEM((eb, tm), u.dtype)],
        compiler_params=_cparams("parallel", "arbitrary"),
        name="peer_ffn",
    )(xt, u, vt, *route)


def _chunk_tri():
    i = np.arange(LANES)
    same = (i[:, None] // HG_CHUNK) == (i[None, :] // HG_CHUNK)
    return jnp.asarray(same & (i[None, :] <= i[:, None]), jnp.bfloat16)


def kernel(x, w_in, hgrn_lb_logits, hgrn_norm_g, s5_lambda_re, s5_lambda_im, s5_log_step,
           s5_b_re, s5_b_im, s5_c_re, s5_c_im, s5_d, s5_w_glu, w_up_a, w_up_b, w_o,
           ln1_g, ln1_b, peer_w_q, peer_keys, peer_u, peer_v, ln2_g, ln2_b):
    bsz, n_tok, d = x.shape
    assert bsz == 1 and d == D_MODEL and n_tok % (S5_CHUNK * 8) == 0
    f32 = jnp.float32
    p = jax.nn.softmax(hgrn_lb_logits.astype(f32), axis=0)
    c = jnp.cumsum(p, axis=0)
    lbs = c - c[0:1]
    loglb = jnp.log(lbs)
    log1mlb = jnp.log1p(-lbs)
    tri = _chunk_tri()

    w_in_b, w_glu_b, w_ua_b, w_ub_b, w_o_b, w_q_b, u_b = (
        _bf(w) for w in (w_in, s5_w_glu, w_up_a, w_up_b, w_o, peer_w_q, peer_u))
    vt_b = _bf(jnp.swapaxes(peer_v, 1, 2))
    row = lambda p, l: p[l][None].astype(f32)

    xf = x.reshape(n_tok, d).astype(f32)
    xb = _bf(xf)
    for l in range(DEPTH):
        proj = _matmul(xb, w_in_b, l, f32, tm=MM_ROWS, tn=MM_COLS)
        oa = _hgrn2(proj, loglb[l:l + 1], log1mlb[l:l + 1], row(hgrn_norm_g, l), tri, tb=HG_ROWS)
        tables = _s5_tables(s5_lambda_re[l], s5_lambda_im[l], s5_log_step[l], s5_b_re[l],
                            s5_b_im[l], s5_c_re[l], s5_c_im[l], s5_d[l])
        yb = _s5(proj, tables, cb=S5_SCAN_CHUNKS, nrb=S5_ROW_BLOCKS)
        ob = _glu(yb, w_glu_b, l, tm=EPI_ROWS, tn=EPI_COLS)
        merged = _up_merge(oa, ob, w_ua_b, w_ub_b, l, proj, tm=EPI_ROWS, tn=EPI_COLS)
        xf, xb = _wo_ln(merged, w_o_b, l, xf, row(ln1_g, l), row(ln1_b, l), tm=LN_ROWS)
        q = _matmul(xb, w_q_b, l, jnp.bfloat16, tm=MM_ROWS, tn=MM_COLS)
        route = _peer_route(q, peer_keys[l].astype(f32), tt=ROUTE_TOKENS)
        y = _peer_ffn(xb.T, u_b, vt_b, l, route, tm=PEER_TOKENS, eb=PEER_EXPERTS)
        xf, xb = _add_ln(xf, y, row(ln2_g, l), row(ln2_b, l), tm=LN_ROWS)
    return xf.reshape(bsz, n_tok, d).astype(x.dtype)
```

```python
import math

import jax
import jax.numpy as jnp
import numpy as np
from jax import lax
from jax.experimental import pallas as pl
from jax.experimental.pallas import tpu as pltpu

D_MODEL = 2048
DEPTH = 4
A_HEADS = 8
A_DIM = 128
A_WIDTH = A_HEADS * A_DIM
HG_CHUNK = 16
HG_UNROLL = 8
B_GROUPS = 64
B_GROUP_CH = 16
B_STATE = 64
B_WIDTH = B_GROUPS * B_GROUP_CH
S5_CHUNK = 16
S5_SLAB_GROUPS = 8
S5_EIG_CLIP = -1e-4
IN_COLS = 4 * A_WIDTH + B_WIDTH + 2 * D_MODEL
P_HEADS = 8
P_QDIM = 256
P_HALF = 128
P_NKEYS = 128
P_NEXP = P_NKEYS * P_NKEYS
P_TOPK = 16
ALPHA = (2.0 * DEPTH) ** 0.25
LN_EPS = 1e-5
RMS_EPS = 1e-6

LANES = 128
GATE_COLS = 128
VMEM_LIMIT = 52 * 1024 * 1024

MM_ROWS, MM_COLS = 1024, 1024
EPI_ROWS, EPI_COLS = 1024, 512
LN_ROWS = 512
HG_ROWS = 1024
S5_SCAN_CHUNKS = 64
S5_ROW_BLOCKS = 2
ROUTE_TOKENS = 128
PEER_TOKENS, PEER_EXPERTS = 512, 1024

_NT = (((1,), (1,)), ((), ()))
_TN = (((0,), (0,)), ((), ()))


def _cparams(*sem, fuse_inputs=None):
    return pltpu.CompilerParams(dimension_semantics=sem, vmem_limit_bytes=VMEM_LIMIT,
                                allow_input_fusion=fuse_inputs)


def _bf(x):
    return x.astype(jnp.bfloat16)


def _dot(a, b):
    return jnp.dot(a, b, preferred_element_type=jnp.float32)


def _sigmoid(x):
    return 1.0 / (1.0 + jnp.exp(-x))


def _gelu(x):
    c = math.sqrt(2.0 / math.pi)
    return 0.5 * x * (1.0 + jnp.tanh(c * (x + 0.044715 * (x * x * x))))


def _layer_norm(z, g, b):
    mu = jnp.mean(z, axis=-1, keepdims=True)
    zc = z - mu
    var = jnp.mean(zc * zc, axis=-1, keepdims=True)
    return zc * lax.rsqrt(var + LN_EPS) * g + b


def _matmul_kernel(a_ref, w_ref, o_ref):
    o_ref[...] = _dot(a_ref[...], w_ref[...]).astype(o_ref.dtype)


def _matmul(a, w, layer, out_dtype, tm, tn):
    m, k = a.shape
    n = w.shape[2]
    tm = min(tm, m)
    return pl.pallas_call(
        _matmul_kernel,
        grid=(n // tn, m // tm),
        in_specs=[pl.BlockSpec((tm, k), lambda j, i: (i, 0)),
                  pl.BlockSpec((None, k, tn), lambda j, i: (layer, 0, j))],
        out_specs=pl.BlockSpec((tm, tn), lambda j, i: (i, j)),
        out_shape=jax.ShapeDtypeStruct((m, n), out_dtype),
        compiler_params=_cparams("parallel", "parallel", fuse_inputs=[False, True]),
        name="matmul",
    )(a, w)


def _hgrn2_kernel(q_ref, f_ref, i_ref, g_ref, loglb_ref, log1mlb_ref, ng_ref, tri_ref,
                  o_ref, st_ref, b_ref, k_ref):
    tb = q_ref.shape[0]

    @pl.when(pl.program_id(1) == 0)
    def _():
        st_ref[...] = jnp.zeros_like(st_ref)

    fz = f_ref[...]
    log_sig = jnp.minimum(fz, 0.0) - jnp.log1p(jnp.exp(-jnp.abs(fz)))
    z = log1mlb_ref[...] + log_sig
    a = loglb_ref[...]
    logf = jnp.maximum(a, z) + jnp.log1p(jnp.exp(-jnp.abs(a - z)))
    k_ref[...] = 1.0 - jnp.exp(logf)
    tri = tri_ref[...]
    for r in range(tb // LANES):
        rows = slice(r * LANES, (r + 1) * LANES)
        lf = logf[rows]
        hi = _bf(lf)
        r1 = lf - hi.astype(jnp.float32)
        mid = _bf(r1)
        lo = _bf(r1 - mid.astype(jnp.float32))
        b_ref[rows, :] = _dot(tri, hi) + _dot(tri, mid) + _dot(tri, lo)

    srow = lax.broadcasted_iota(jnp.int32, (HG_CHUNK, 1), 0)
    ng = ng_ref[...]

    def chunk(j, st):
        rows = pl.ds(pl.multiple_of(j * HG_CHUNK, HG_CHUNK), HG_CHUNK)
        b = b_ref[rows, :]
        q = q_ref[rows, :]
        k = k_ref[rows, :]
        v = i_ref[rows, :]
        b_last = b[HG_CHUNK - 1:HG_CHUNK, :]
        o = lax.dot_general(_bf(q * jnp.exp(b)), _bf(st), _NT,
                            preferred_element_type=jnp.float32)
        intra = []
        for t in range(HG_CHUNK):
            diff = jnp.where(srow <= t, b[t:t + 1, :] - b, -jnp.inf)
            p = (q[t:t + 1, :] * k) * jnp.exp(diff)
            att = jnp.sum(p, axis=-1, keepdims=True)
            intra.append(jnp.sum(att * v, axis=0, keepdims=True))
        o = o + jnp.concatenate(intra, axis=0)
        k_tail = k * jnp.exp(b_last - b)
        st = st * jnp.exp(b_last) + lax.dot_general(
            _bf(v), _bf(k_tail), _TN, preferred_element_type=jnp.float32)
        o = o * lax.rsqrt(jnp.mean(o * o, axis=-1, keepdims=True) + RMS_EPS)
        o = o * ng * _sigmoid(g_ref[rows, :])
        o_ref[rows, :] = o.astype(o_ref.dtype)
        return st

    st_ref[...] = lax.fori_loop(0, tb // HG_CHUNK, chunk, st_ref[...], unroll=HG_UNROLL)


def _hgrn2(proj, loglb, log1mlb, norm_g, tri, tb):
    n_tok = proj.shape[0]
    tb = min(tb, n_tok)

    def col(off):
        return pl.BlockSpec((tb, A_DIM), lambda h, i: (i, off + h))

    def par():
        return pl.BlockSpec((1, A_DIM), lambda h, i: (0, h))

    return pl.pallas_call(
        _hgrn2_kernel,
        grid=(A_HEADS, n_tok // tb),
        in_specs=[col(0), col(A_HEADS), col(2 * A_HEADS), col(3 * A_HEADS),
                  par(), par(), par(),
                  pl.BlockSpec((LANES, LANES), lambda h, i: (0, 0))],
        out_specs=pl.BlockSpec((tb, A_DIM), lambda h, i: (i, h)),
        out_shape=jax.ShapeDtypeStruct((n_tok, A_WIDTH), jnp.bfloat16),
        scratch_shapes=[pltpu.VMEM((A_DIM, A_DIM), jnp.float32),
                        pltpu.VMEM((tb, A_DIM), jnp.float32),
                        pltpu.VMEM((tb, A_DIM), jnp.float32)],
        compiler_params=_cparams("parallel", "arbitrary"),
        name="hgrn2",
    )(proj, proj, proj, proj, loglb, log1mlb, norm_g, tri)


def _s5_tables(lam_re, lam_im, log_step, b_re, b_im, c_re, c_im, d):
    f32 = jnp.float32
    hp = lax.Precision.HIGHEST
    t = S5_CHUNK
    lr = jnp.minimum(lam_re.astype(f32), S5_EIG_CLIP)
    li = lam_im.astype(f32)
    dt = jnp.exp(log_step.astype(f32))[:, None]
    mag = jnp.exp(lr * dt)
    ar = mag * jnp.cos(li * dt)
    ai = mag * jnp.sin(li * dt)
    den = lr * lr + li * li
    nr = ar - 1.0
    zr = (nr * lr + ai * li) / den
    zi = (ai * lr - nr * li) / den
    br_, bi_ = b_re.astype(f32), b_im.astype(f32)
    bbr = zr[..., None] * br_ - zi[..., None] * bi_
    bbi = zr[..., None] * bi_ + zi[..., None] * br_
    pr = [jnp.ones_like(ar)]
    pi = [jnp.zeros_like(ai)]
    for _ in range(t):
        r0, i0 = pr[-1], pi[-1]
        pr.append(r0 * ar - i0 * ai)
        pi.append(r0 * ai + i0 * ar)
    pwr = jnp.stack(pr, axis=1)
    pwi = jnp.stack(pi, axis=1)
    cr, ci = c_re.astype(f32), c_im.astype(f32)
    car = cr[:, None] * pwr[:, :t, None, :] - ci[:, None] * pwi[:, :t, None, :]
    cai = cr[:, None] * pwi[:, :t, None, :] + ci[:, None] * pwr[:, :t, None, :]
    kk = (jnp.einsum('gtnp,gpm->gtnm', car, bbr, precision=hp)
          - jnp.einsum('gtnp,gpm->gtnm', cai, bbi, precision=hp))
    ti = np.arange(t)
    sg = S5_SLAB_GROUPS
    ns = B_GROUPS // sg
    ch = B_GROUP_CH
    ns2 = 2 * B_STATE
    shift = np.zeros((t, ch, t, t, ch), np.float32)
    for t0 in range(t):
        for t1 in range(t0, t):
            shift[t1 - t0, :, t0, t1, :] = np.eye(ch)
    kmat = jnp.transpose(kk, (0, 3, 1, 2)).reshape(ns, sg * ch, t * ch)
    toep = jnp.einsum('srk,ktc->strc', _bf(kmat),
                      jnp.asarray(shift.reshape(t * ch, t, t * ch), jnp.bfloat16),
                      preferred_element_type=jnp.bfloat16)
    toep = toep.reshape(ns, t * LANES, t * ch)
    er = pwr[:, t - 1 - ti][:, :, None, :]
    ei = pwi[:, t - 1 - ti][:, :, None, :]
    bbr_t = jnp.transpose(bbr, (0, 2, 1))[:, None]
    bbi_t = jnp.transpose(bbi, (0, 2, 1))[:, None]
    w1 = jnp.concatenate([er * bbr_t - ei * bbi_t, er * bbi_t + ei * bbr_t], axis=-1)
    w1 = jnp.transpose(w1.reshape(ns, sg, t, ch, ns2), (0, 2, 1, 3, 4))
    w1 = _bf(w1.reshape(ns, t * LANES, ns2))
    zr1 = pwr[:, 1:, None, :]
    zi1 = pwi[:, 1:, None, :]
    w2r = cr[:, None] * zr1 - ci[:, None] * zi1
    w2i = -(cr[:, None] * zi1 + ci[:, None] * zr1)
    w2 = jnp.concatenate([w2r, w2i], axis=-1)
    w2 = jnp.transpose(w2.reshape(ns, sg, t, ch, ns2), (0, 1, 4, 2, 3))
    w2 = _bf(w2.reshape(ns, sg * ns2, t * ch))
    a1 = jnp.concatenate([pwr[:, t], pwr[:, t]], axis=-1)
    a2 = jnp.concatenate([-pwi[:, t], pwi[:, t]], axis=-1)
    dd = jnp.tile(d.astype(f32).reshape(ns, 1, LANES), (1, 1, t))
    return toep, w1, w2, a1, a2, dd


def _s5_spread_consts():
    t, sg, ch, ns2 = S5_CHUNK, S5_SLAB_GROUPS, B_GROUP_CH, 2 * B_STATE
    grp = np.arange(LANES) // ch
    col_un = np.kron(np.eye(t), np.kron(np.ones((1, sg)), np.eye(ch)))
    col_p = np.kron(np.ones((1, sg)), np.eye(ns2))
    keep_un = grp[:, None] == np.tile(grp, t)[None, :]
    keep_p = grp[:, None] == np.repeat(np.arange(sg), ns2)[None, :]
    keep_g = np.arange(sg)[:, None] == np.tile(grp, t)[None, :]
    bf = lambda m: jnp.asarray(m, jnp.bfloat16)
    f = lambda m: jnp.asarray(m, jnp.float32)
    return bf(col_un), bf(col_p), f(keep_un), f(keep_p), f(keep_g)


def _s5_rows(u_ref, nc):
    return jnp.concatenate([u_ref[pl.ds(t, nc, stride=S5_CHUNK), :] for t in range(S5_CHUNK)],
                           axis=1)


def _s5_local_kernel(u_ref, w1c_ref, colp_ref, keepp_ref, e_ref, w1_ref):
    @pl.when(pl.program_id(1) == 0)
    def _():
        for t in range(S5_CHUNK):
            rows = slice(t * LANES, (t + 1) * LANES)
            w1_ref[rows, :] = _bf(_dot(w1c_ref[0, rows, :], colp_ref[...]) * keepp_ref[...])

    e_ref[...] = _dot(_bf(_s5_rows(u_ref, e_ref.shape[0])), w1_ref[...])


def _s5_scan_kernel(e_ref, a1_ref, a2_ref, xp_ref, st_ref):
    @pl.when(pl.program_id(0) == 0)
    def _():
        st_ref[...] = jnp.zeros_like(st_ref)

    a1 = a1_ref[...]
    a2 = a2_ref[...]

    def step(c, carry):
        x, xs = carry
        xp_ref[c] = x
        e = e_ref[c]
        return a1 * x + a2 * xs + e, a1 * xs - a2 * x + pltpu.roll(e, B_STATE, 1)

    x0 = st_ref[...]
    x, _ = lax.fori_loop(0, e_ref.shape[0], step, (x0, pltpu.roll(x0, B_STATE, 1)), unroll=8)
    st_ref[...] = x


def _s5_out_kernel(u_ref, xp_ref, tc_ref, w2c_ref, colun_ref, keepun_ref, keepg_ref, d_ref,
                   y_ref, toep_ref, w2_ref):
    nc = xp_ref.shape[0]

    @pl.when(pl.program_id(1) == 0)
    def _():
        for t in range(S5_CHUNK):
            rows = slice(t * LANES, (t + 1) * LANES)
            toep_ref[rows, :] = _bf(_dot(tc_ref[0, rows, :], colun_ref[...]) * keepun_ref[...])
        for g in range(S5_SLAB_GROUPS):
            rows = slice(g * 2 * B_STATE, (g + 1) * 2 * B_STATE)
            w2_ref[rows, :] = _bf(_dot(w2c_ref[0, rows, :], colun_ref[...]) * keepg_ref[g:g + 1, :])

    x = _s5_rows(u_ref, nc)
    y = _dot(_bf(x), toep_ref[...]) + _dot(_bf(xp_ref[...]), w2_ref[...]) + d_ref[0] * x
    y = _gelu(y)
    for t in range(S5_CHUNK):
        y_ref[pl.ds(t, nc, stride=S5_CHUNK), :] = y[:, t * LANES:(t + 1) * LANES]


def _s5(proj, tables, cb, nrb):
    toep, w1, w2, a1, a2, dd = tables
    col_un, col_p, keep_un, keep_p, keep_g = _s5_spread_consts()
    n_tok = proj.shape[0]
    nc = n_tok // S5_CHUNK
    g = B_GROUPS
    ns = 2 * B_STATE
    n_slab = B_WIDTH // LANES
    sw = S5_SLAB_GROUPS * ns
    rw = S5_CHUNK * LANES
    u0 = 4 * A_WIDTH // LANES
    cb = min(cb, nc)
    ncb = nc // nrb
    tokb = n_tok // nrb
    fixed = lambda shape: pl.BlockSpec(shape, lambda s, r: (0,) * len(shape))
    slab = lambda shape: pl.BlockSpec((1,) + shape, lambda s, r: (s, 0, 0))
    e = pl.pallas_call(
        _s5_local_kernel,
        grid=(n_slab, nrb),
        in_specs=[pl.BlockSpec((tokb, LANES), lambda s, r: (r, u0 + s)),
                  slab(w1.shape[1:]), fixed(col_p.shape), fixed(keep_p.shape)],
        out_specs=pl.BlockSpec((ncb, sw), lambda s, r: (r, s)),
        out_shape=jax.ShapeDtypeStruct((nc, g * ns), jnp.float32),
        scratch_shapes=[pltpu.VMEM((rw, sw), jnp.bfloat16)],
        compiler_params=_cparams("parallel", "arbitrary"),
        name="s5_local",
    )(proj, w1, col_p, keep_p)
    xp = pl.pallas_call(
        _s5_scan_kernel,
        grid=(nc // cb,),
        in_specs=[pl.BlockSpec((cb, g, ns), lambda i: (i, 0, 0)),
                  pl.BlockSpec((g, ns), lambda i: (0, 0)),
                  pl.BlockSpec((g, ns), lambda i: (0, 0))],
        out_specs=pl.BlockSpec((cb, g, ns), lambda i: (i, 0, 0)),
        out_shape=jax.ShapeDtypeStruct((nc, g, ns), jnp.float32),
        scratch_shapes=[pltpu.VMEM((g, ns), jnp.float32)],
        compiler_params=_cparams("arbitrary"),
        name="s5_scan",
    )(e.reshape(nc, g, ns), a1, a2)
    return pl.pallas_call(
        _s5_out_kernel,
        grid=(n_slab, nrb),
        in_specs=[pl.BlockSpec((tokb, LANES), lambda s, r: (r, u0 + s)),
                  pl.BlockSpec((ncb, sw), lambda s, r: (r, s)),
                  slab(toep.shape[1:]), slab(w2.shape[1:]),
                  fixed(col_un.shape), fixed(keep_un.shape), fixed(keep_g.shape),
                  slab((1, rw))],
        out_specs=pl.BlockSpec((tokb, LANES), lambda s, r: (r, s)),
        out_shape=jax.ShapeDtypeStruct((n_tok, B_WIDTH), jnp.float32),
        scratch_shapes=[pltpu.VMEM((rw, rw), jnp.bfloat16),
                        pltpu.VMEM((sw, rw), jnp.bfloat16)],
        compiler_params=_cparams("parallel", "arbitrary"),
        name="s5_out",
    )(proj, xp.reshape(nc, g * ns), toep, w2, col_un, keep_un, keep_g, dd)


def _glu_kernel(y_ref, wa_ref, wb_ref, o_ref):
    y = _bf(y_ref[...])
    o_ref[...] = (_dot(y, wa_ref[...]) * _sigmoid(_dot(y, wb_ref[...]))).astype(o_ref.dtype)


def _glu(y, w, layer, tm, tn):
    m, k = y.shape
    n = w.shape[2] // 2
    tm = min(tm, m)
    nb = n // tn
    return pl.pallas_call(
        _glu_kernel,
        grid=(nb, m // tm),
        in_specs=[pl.BlockSpec((tm, k), lambda j, i: (i, 0)),
                  pl.BlockSpec((None, k, tn), lambda j, i: (layer, 0, j)),
                  pl.BlockSpec((None, k, tn), lambda j, i: (layer, 0, nb + j))],
        out_specs=pl.BlockSpec((tm, tn), lambda j, i: (i, j)),
        out_shape=jax.ShapeDtypeStruct((m, n), jnp.bfloat16),
        compiler_params=_cparams("parallel", "parallel", fuse_inputs=[False, True, True]),
        name="glu",
    )(y, w, w)


def _up_merge_kernel(oa_ref, ob_ref, wa_ref, wb_ref, ga_ref, gb_ref, o_ref):
    m = (_sigmoid(ga_ref[...]) * _dot(oa_ref[...], wa_ref[...])
         + _sigmoid(gb_ref[...]) * _dot(ob_ref[...], wb_ref[...]))
    o_ref[...] = m.astype(o_ref.dtype)


def _up_merge(oa, ob, wa, wb, layer, proj, tm, tn):
    m, k = oa.shape
    n = wa.shape[2]
    tm = min(tm, m)
    ga0 = (4 * A_WIDTH + B_WIDTH) // tn
    gb0 = ga0 + D_MODEL // tn
    return pl.pallas_call(
        _up_merge_kernel,
        grid=(n // tn, m // tm),
        in_specs=[pl.BlockSpec((tm, k), lambda j, i: (i, 0)),
                  pl.BlockSpec((tm, k), lambda j, i: (i, 0)),
                  pl.BlockSpec((None, k, tn), lambda j, i: (layer, 0, j)),
                  pl.BlockSpec((None, k, tn), lambda j, i: (layer, 0, j)),
                  pl.BlockSpec((tm, tn), lambda j, i: (i, ga0 + j)),
                  pl.BlockSpec((tm, tn), lambda j, i: (i, gb0 + j))],
        out_specs=pl.BlockSpec((tm, tn), lambda j, i: (i, j)),
        out_shape=jax.ShapeDtypeStruct((m, n), jnp.bfloat16),
        compiler_params=_cparams("parallel", "parallel",
                                 fuse_inputs=[False, False, True, True, False, False]),
        name="up_merge",
    )(oa, ob, wa, wb, proj, proj)


def _wo_ln_kernel(m_ref, w_ref, x_ref, g_ref, b_ref, o_ref, ob_ref, obt_ref):
    z = ALPHA * x_ref[...] + _dot(m_ref[...], w_ref[...])
    y = _layer_norm(z, g_ref[...], b_ref[...])
    o_ref[...] = y
    ob_ref[...] = _bf(y)
    obt_ref[...] = _bf(y.T)


def _wo_ln(merged, w, layer, x, g, b, tm):
    m, k = merged.shape
    n = w.shape[2]
    tm = min(tm, m)
    row = lambda i: (i, 0)
    fix = lambda i: (0, 0)
    return pl.pallas_call(
        _wo_ln_kernel,
        grid=(m // tm,),
        in_specs=[pl.BlockSpec((tm, k), row), pl.BlockSpec((None, k, n), lambda i: (layer, 0, 0)),
                  pl.BlockSpec((tm, n), row), pl.BlockSpec((1, n), fix),
                  pl.BlockSpec((1, n), fix)],
        out_specs=[pl.BlockSpec((tm, n), row), pl.BlockSpec((tm, n), row),
                   pl.BlockSpec((n, tm), lambda i: (0, i))],
        out_shape=[jax.ShapeDtypeStruct((m, n), jnp.float32),
                   jax.ShapeDtypeStruct((m, n), jnp.bfloat16),
                   jax.ShapeDtypeStruct((n, m), jnp.bfloat16)],
        compiler_params=_cparams("parallel", fuse_inputs=[False, True, False, False, False]),
        name="wo_ln",
    )(merged, w, x, g, b)


def _add_ln_kernel(x_ref, y_ref, g_ref, b_ref, o_ref, ob_ref):
    y = _layer_norm(ALPHA * x_ref[...] + y_ref[...], g_ref[...], b_ref[...])
    o_ref[...] = y
    ob_ref[...] = _bf(y)


def _add_ln(x, y, g, b, tm):
    m, n = x.shape
    tm = min(tm, m)
    row = lambda i: (i, 0)
    fix = lambda i: (0, 0)
    return pl.pallas_call(
        _add_ln_kernel,
        grid=(m // tm,),
        in_specs=[pl.BlockSpec((tm, n), row), pl.BlockSpec((tm, n), row),
                  pl.BlockSpec((1, n), fix), pl.BlockSpec((1, n), fix)],
        out_specs=[pl.BlockSpec((tm, n), row), pl.BlockSpec((tm, n), row)],
        out_shape=[jax.ShapeDtypeStruct((m, n), jnp.float32),
                   jax.ShapeDtypeStruct((m, n), jnp.bfloat16)],
        compiler_params=_cparams("parallel"),
        name="add_ln",
    )(x, y, g, b)


def _take_top(s, n_take, on_take):
    rows = s.shape[0]
    rid = lax.broadcasted_iota(jnp.int32, s.shape, 0)
    for k in range(n_take):
        m = jnp.max(s, axis=0, keepdims=True)
        idx = jnp.min(jnp.where(s == m, rid, rows), axis=0, keepdims=True)
        on_take(k, m, idx)
        s = jnp.where(rid == idx, -jnp.inf, s)


def _peer_route_kernel(q_ref, keys_ref, r2_ref, e2_ref, n_ref, c_ref):
    tt = q_ref.shape[0]
    kid = lax.broadcasted_iota(jnp.int32, (P_NKEYS, tt), 0)
    rank_id = lax.broadcasted_iota(jnp.int32, (P_TOPK, tt), 0)
    for h in range(P_HEADS):
        s = []
        for half in range(2):
            qh = q_ref[:, (2 * h + half) * P_HALF:(2 * h + half + 1) * P_HALF]
            s.append(lax.dot_general(_bf(keys_ref[h, half]), _bf(qh), _NT,
                                     preferred_element_type=jnp.float32))
        vals = [[], []]
        idxs = [[], []]
        rank2 = [jnp.full((P_NKEYS, tt), P_TOPK, jnp.int32)]
        for half in range(2):
            def take(k, m, idx, half=half):
                vals[half].append(m)
                idxs[half].append(idx)
                if half == 1:
                    rank2[0] = jnp.where(kid == idx, k, rank2[0])
            _take_top(s[half], P_TOPK, take)
        v1 = jnp.concatenate(vals[0], axis=0)
        v2 = jnp.concatenate(vals[1], axis=0)
        widths = [P_TOPK // (i + 1) for i in range(P_TOPK)]
        starts = np.cumsum([0] + widths)
        pad = int(-starts[-1] % 8)
        cand = jnp.concatenate([v1[i:i + 1] + v2[:widths[i]] for i in range(P_TOPK)]
                               + [jnp.full((pad, tt), -jnp.inf, jnp.float32)], axis=0)
        state = [jnp.zeros((P_TOPK, tt), jnp.int32), jnp.zeros((1, tt), jnp.float32)]
        top = v1[0:1] + v2[0:1]

        def take_c(k, m, pos):
            rank1 = sum((pos >= int(st)).astype(jnp.int32) for st in starts[1:P_TOPK])
            state[0] = state[0] + (rank_id == rank1).astype(jnp.int32)
            state[1] = state[1] + jnp.exp(m - top)
        _take_top(cand, P_TOPK, take_c)
        n_rank, z = state
        n_key = jnp.zeros((P_NKEYS, tt), jnp.int32)
        for i in range(P_TOPK):
            n_key = jnp.where(kid == idxs[0][i], n_rank[i:i + 1], n_key)
        r2_ref[h] = rank2[0].astype(jnp.float32).astype(r2_ref.dtype)
        n_ref[h] = n_key.astype(jnp.float32)
        e2_ref[h] = jnp.exp(s[1] - v2[0:1]).astype(e2_ref.dtype)
        c_ref[h] = jnp.exp(s[0] - v1[0:1]) / z


def _peer_route(q, keys, tt, gate_dtype=jnp.bfloat16):
    n_tok = q.shape[0]
    tt = min(tt, n_tok)
    out = [jax.ShapeDtypeStruct((P_HEADS, P_NKEYS, n_tok), dt)
           for dt in (gate_dtype, gate_dtype, jnp.float32, jnp.float32)]
    ospec = pl.BlockSpec((P_HEADS, P_NKEYS, tt), lambda i: (0, 0, i))
    return pl.pallas_call(
        _peer_route_kernel,
        grid=(n_tok // tt,),
        in_specs=[pl.BlockSpec((tt, P_HEADS * P_QDIM), lambda i: (i, 0)),
                  pl.BlockSpec((P_HEADS, 2, P_NKEYS, P_HALF), lambda i: (0, 0, 0, 0))],
        out_specs=[ospec] * 4,
        out_shape=out,
        compiler_params=_cparams("parallel"),
        name="peer_route",
    )(q, keys)


def _peer_ffn_kernel(xt_ref, u_ref, vt_ref, r2_ref, e2_ref, n_ref, c_ref, y_ref, acc_ref, h_ref):
    j = pl.program_id(1)
    eb = u_ref.shape[0]
    tm = xt_ref.shape[1]

    @pl.when(j == 0)
    def _():
        acc_ref[...] = jnp.zeros_like(acc_ref)

    act = _dot(u_ref[...], xt_ref[...])
    gdt = h_ref.dtype
    pack = 8 * 4 // jnp.dtype(gdt).itemsize
    for al in range(eb // P_NKEYS):
        a = j * (eb // P_NKEYS) + al
        rows = slice(al * P_NKEYS, (al + 1) * P_NKEYS)
        for cc in range(tm // GATE_COLS):
            cols = slice(cc * GATE_COLS, (cc + 1) * GATE_COLS)
            gate = None
            for h in range(P_HEADS):
                n_tile = jnp.broadcast_to(n_ref[h, pl.ds(a, 1), :][:, cols], (pack, GATE_COLS)).astype(gdt)
                c_tile = jnp.broadcast_to(c_ref[h, pl.ds(a, 1), :][:, cols], (pack, GATE_COLS)).astype(gdt)
                r2 = r2_ref[h, :, cols].reshape(P_NKEYS // pack, pack, GATE_COLS)
                e2 = e2_ref[h, :, cols].reshape(P_NKEYS // pack, pack, GATE_COLS)
                term = jnp.where(r2 < n_tile[None], e2 * c_tile[None], jnp.zeros((), gdt))
                gate = term if gate is None else gate + term
            g_act = _gelu(act[rows, cols].astype(gdt)).reshape(P_NKEYS // pack, pack, GATE_COLS)
            h_ref[rows, cols] = (g_act * gate).reshape(P_NKEYS, GATE_COLS)
    acc_ref[...] += _dot(vt_ref[...], h_ref[...])

    @pl.when(j == pl.num_programs(1) - 1)
    def _():
        y_ref[...] = acc_ref[...].T


def _peer_ffn(xt, u, vt, layer, route, tm, eb):
    d, n_tok = xt.shape
    tm = min(tm, n_tok)
    rspec = pl.BlockSpec((P_HEADS, P_NKEYS, tm), lambda i, j: (0, 0, i))
    return pl.pallas_call(
        _peer_ffn_kernel,
        grid=(n_tok // tm, P_NEXP // eb),
        in_specs=[pl.BlockSpec((d, tm), lambda i, j: (0, i)),
                  pl.BlockSpec((None, eb, d), lambda i, j: (layer, j, 0)),
                  pl.BlockSpec((None, d, eb), lambda i, j: (layer, 0, j)),
                  rspec, rspec, rspec, rspec],
        out_specs=pl.BlockSpec((tm, d), lambda i, j: (i, 0)),
        out_shape=jax.ShapeDtypeStruct((n_tok, d), jnp.float32),
        scratch_shapes=[pltpu.VMEM((d, tm), jnp.float32),
                        pltpu.VMEM((eb, tm), u.dtype)],
        compiler_params=_cparams("parallel", "arbitrary"),
        name="peer_ffn",
    )(xt, u, vt, *route)


def _chunk_tri():
    i = np.arange(LANES)
    same = (i[:, None] // HG_CHUNK) == (i[None, :] // HG_CHUNK)
    return jnp.asarray(same & (i[None, :] <= i[:, None]), jnp.bfloat16)


def kernel(x, w_in, hgrn_lb_logits, hgrn_norm_g, s5_lambda_re, s5_lambda_im, s5_log_step,
           s5_b_re, s5_b_im, s5_c_re, s5_c_im, s5_d, s5_w_glu, w_up_a, w_up_b, w_o,
           ln1_g, ln1_b, peer_w_q, peer_keys, peer_u, peer_v, ln2_g, ln2_b):
    bsz, n_tok, d = x.shape
    assert bsz == 1 and d == D_MODEL and n_tok % (S5_CHUNK * 8) == 0
    f32 = jnp.float32
    p = jax.nn.softmax(hgrn_lb_logits.astype(f32), axis=0)
    c = jnp.cumsum(p, axis=0)
    lbs = c - c[0:1]
    loglb = jnp.log(lbs)
    log1mlb = jnp.log1p(-lbs)
    tri = _chunk_tri()

    w_in_b, w_glu_b, w_ua_b, w_ub_b, w_o_b, w_q_b, u_b = (
        _bf(w) for w in (w_in, s5_w_glu, w_up_a, w_up_b, w_o, peer_w_q, peer_u))
    vt_b = _bf(jnp.swapaxes(peer_v, 1, 2))
    row = lambda p, l: p[l][None].astype(f32)

    xf = x.reshape(n_tok, d).astype(f32)
    xb = _bf(xf)
    for l in range(DEPTH):
        proj = _matmul(xb, w_in_b, l, f32, tm=MM_ROWS, tn=MM_COLS)
        oa = _hgrn2(proj, loglb[l:l + 1], log1mlb[l:l + 1], row(hgrn_norm_g, l), tri, tb=HG_ROWS)
        tables = _s5_tables(s5_lambda_re[l], s5_lambda_im[l], s5_log_step[l], s5_b_re[l],
                            s5_b_im[l], s5_c_re[l], s5_c_im[l], s5_d[l])
        yb = _s5(proj, tables, cb=S5_SCAN_CHUNKS, nrb=S5_ROW_BLOCKS)
        ob = _glu(yb, w_glu_b, l, tm=EPI_ROWS, tn=EPI_COLS)
        merged = _up_merge(oa, ob, w_ua_b, w_ub_b, l, proj, tm=EPI_ROWS, tn=EPI_COLS)
        xf, xb, xt = _wo_ln(merged, w_o_b, l, xf, row(ln1_g, l), row(ln1_b, l), tm=LN_ROWS)
        q = _matmul(xb, w_q_b, l, jnp.bfloat16, tm=MM_ROWS, tn=MM_COLS)
        route = _peer_route(q, peer_keys[l].astype(f32), tt=ROUTE_TOKENS)
        y = _peer_ffn(xt, u_b, vt_b, l, route, tm=PEER_TOKENS, eb=PEER_EXPERTS)
        xf, xb = _add_ln(xf, y, row(ln2_g, l), row(ln2_b, l), tm=LN_ROWS)
    return xf.reshape(bsz, n_tok, d).astype(x.dtype)
```
